```python
import math
import jax, jax.numpy as jnp
from jax import lax
import numpy as np

D_MODEL = 1024
BATCH = 8
SEQ = 2048
DEPTH = 4

N_A_LAYERS = DEPTH // 2
N_B_LAYERS = DEPTH - N_A_LAYERS
CONV_WIDTH = 31
N_HEADS = 8
HEAD_DIM = D_MODEL // N_HEADS // 2
V_DIM = 2 * HEAD_DIM
QK_WIDTH = N_HEADS * 2 * HEAD_DIM
V_WIDTH = N_HEADS * V_DIM
D_FF = -(-(8 * D_MODEL) // (3 * 256)) * 256
Q_BLOCK = 128
EPS = 1e-6

kernel_name = "yoco_conformer_diffattn_alibi"


def rmsnorm(x, g):
    x32 = x.astype(jnp.float32)
    y = x32 * lax.rsqrt(jnp.mean(x32 * x32, axis=-1, keepdims=True) + EPS)
    return (y * g.astype(jnp.float32)).astype(x.dtype)


def layernorm(x, g, b):
    x32 = x.astype(jnp.float32)
    mu = jnp.mean(x32, axis=-1, keepdims=True)
    xc = x32 - mu
    y = xc * lax.rsqrt(jnp.mean(xc * xc, axis=-1, keepdims=True) + EPS)
    return (y * g.astype(jnp.float32) + b.astype(jnp.float32)).astype(x.dtype)


def alibi_slopes():
    return jnp.asarray((2.0 ** (-8.0 * np.arange(1, N_HEADS + 1) / N_HEADS)).astype(np.float32))


def conformer_conv(h, pw1_w, pw1_b, dw_w, dw_b, ln_g, ln_b, pw2_w, pw2_b):
    u = h @ pw1_w + pw1_b
    a, gate = jnp.split(u, 2, axis=-1)
    u = a * jax.nn.sigmoid(gate)
    u = lax.conv_general_dilated(
        u, dw_w[:, None, :].astype(u.dtype), window_strides=(1,),
        padding=((CONV_WIDTH - 1, 0),),
        dimension_numbers=("NWC", "WIO", "NWC"),
        feature_group_count=D_MODEL) + dw_b
    u = jax.nn.silu(layernorm(u, ln_g, ln_b))
    return u @ pw2_w + pw2_b


def diff_attention(h, k, v, w_q, lam_q1, lam_k1, lam_q2, lam_k2, subln_g, w_o, lambda_init):
    B, S, _ = h.shape
    q = (h @ w_q).reshape(B, S, N_HEADS, 2, HEAD_DIM)
    lam = (jnp.exp(jnp.sum(lam_q1.astype(jnp.float32) * lam_k1.astype(jnp.float32)))
           - jnp.exp(jnp.sum(lam_q2.astype(jnp.float32) * lam_k2.astype(jnp.float32)))
           + lambda_init)
    slopes = alibi_slopes()
    scale = HEAD_DIM ** -0.5
    outs = []
    for i in range(S // Q_BLOCK):
        end = (i + 1) * Q_BLOCK
        q_blk = q[:, i * Q_BLOCK:end]
        k_pre = k[:, :end]
        v_pre = v[:, :end]
        s = jnp.einsum("bqhcd,bkhcd->bhcqk", q_blk, k_pre,
                       preferred_element_type=jnp.float32) * scale
        t_pos = i * Q_BLOCK + jnp.arange(Q_BLOCK)
        s_pos = jnp.arange(end)
        dist = (t_pos[:, None] - s_pos[None, :]).astype(jnp.float32)
        bias = jnp.where(dist >= 0, -slopes[:, None, None] * dist, -jnp.inf)
        p = jax.nn.softmax(s + bias[None, :, None], axis=-1)
        attn = p[:, :, 0] - lam * p[:, :, 1]
        outs.append(jnp.einsum("bhqk,bkhe->bqhe", attn.astype(v.dtype), v_pre))
    o = jnp.concatenate(outs, axis=1)
    o = rmsnorm(o, subln_g) * (1.0 - lambda_init)
    return o.reshape(B, S, V_WIDTH) @ w_o


def swiglu(h, w1, w3, w2):
    return (jax.nn.silu(h @ w1) * (h @ w3)) @ w2


def setup_inputs(seed: int = 0) -> dict:
    key = jax.random.key(seed)
    ks = iter(jax.random.split(key, 32))
    f32 = jnp.float32

    def nrm(shape, scale):
        return jax.random.normal(next(ks), shape, f32) * scale

    def gain(shape):
        return 1.0 + nrm(shape, 0.02)

    D = D_MODEL
    return {
        "x": nrm((BATCH, SEQ, D), 1.0),
        "conv_norm_g": gain((N_A_LAYERS, D)),
        "pw1_w": nrm((N_A_LAYERS, D, 2 * D), D ** -0.5),
        "pw1_b": nrm((N_A_LAYERS, 2 * D), 0.02),
        "dw_w": nrm((N_A_LAYERS, CONV_WIDTH, D), CONV_WIDTH ** -0.5),
        "dw_b": nrm((N_A_LAYERS, D), 0.02),
        "conv_ln_g": gain((N_A_LAYERS, D)),
        "conv_ln_b": nrm((N_A_LAYERS, D), 0.02),
        "pw2_w": nrm((N_A_LAYERS, D, D), D ** -0.5),
        "pw2_b": nrm((N_A_LAYERS, D), 0.02),
        "kv_norm_g": gain((D,)),
        "w_k": nrm((D, QK_WIDTH), D ** -0.5),
        "w_v": nrm((D, V_WIDTH), D ** -0.5),
        "attn_norm_g": gain((N_B_LAYERS, D)),
        "w_q": nrm((N_B_LAYERS, D, QK_WIDTH), D ** -0.5),
        "lambda_q1": nrm((N_B_LAYERS, HEAD_DIM), 0.1),
        "lambda_k1": nrm((N_B_LAYERS, HEAD_DIM), 0.1),
        "lambda_q2": nrm((N_B_LAYERS, HEAD_DIM), 0.1),
        "lambda_k2": nrm((N_B_LAYERS, HEAD_DIM), 0.1),
        "subln_g": gain((N_B_LAYERS, V_DIM)),
        "w_o": nrm((N_B_LAYERS, V_WIDTH, D), V_WIDTH ** -0.5),
        "ffn_norm_g": gain((DEPTH, D)),
        "ffn_w1": nrm((DEPTH, D, D_FF), D ** -0.5),
        "ffn_w3": nrm((DEPTH, D, D_FF), D ** -0.5),
        "ffn_w2": nrm((DEPTH, D_FF, D), D_FF ** -0.5),
        "final_norm_g": gain((D,)),
    }


def reference(x, conv_norm_g, pw1_w, pw1_b, dw_w, dw_b, conv_ln_g, conv_ln_b, pw2_w, pw2_b,
              kv_norm_g, w_k, w_v, attn_norm_g, w_q, lambda_q1, lambda_k1, lambda_q2, lambda_k2,
              subln_g, w_o, ffn_norm_g, ffn_w1, ffn_w3, ffn_w2, final_norm_g):
    B, S, _ = x.shape
    k = None
    v = None
    for layer in range(DEPTH):
        if layer < N_A_LAYERS:
            x = x + conformer_conv(rmsnorm(x, conv_norm_g[layer]), pw1_w[layer], pw1_b[layer],
                                   dw_w[layer], dw_b[layer], conv_ln_g[layer], conv_ln_b[layer],
                                   pw2_w[layer], pw2_b[layer])
        else:
            j = layer - N_A_LAYERS
            lambda_init = 0.8 - 0.6 * math.exp(-0.3 * layer)
            x = x + diff_attention(rmsnorm(x, attn_norm_g[j]), k, v, w_q[j],
                                   lambda_q1[j], lambda_k1[j], lambda_q2[j], lambda_k2[j],
                                   subln_g[j], w_o[j], lambda_init)
        x = x + swiglu(rmsnorm(x, ffn_norm_g[layer]), ffn_w1[layer], ffn_w3[layer], ffn_w2[layer])
        if layer == N_A_LAYERS - 1:
            kv_in = rmsnorm(x, kv_norm_g)
            k = (kv_in @ w_k).reshape(B, S, N_HEADS, 2, HEAD_DIM)
            v = (kv_in @ w_v).reshape(B, S, N_HEADS, V_DIM)
    return rmsnorm(x, final_norm_g)
```

```python
import functools
import math

import jax
import jax.numpy as jnp
import numpy as np
from jax import lax
from jax.experimental import pallas as pl
from jax.experimental.pallas import tpu as pltpu

F32 = jnp.float32
BF16 = jnp.bfloat16

EPS = 1e-6
CONV_WIDTH = 31
N_HEADS = 8
HEAD_DIM = 64
V_DIM = 2 * HEAD_DIM
HALO = 32
LANES = 128
VMEM_LIMIT = 56 * 1024 * 1024

ROW_TILE = 256
Q_TILE = 256
KV_TILE = 256
MASK_VALUE = -1e30


def _rmsnorm(x, g):
    return x * lax.rsqrt(jnp.mean(x * x, axis=-1, keepdims=True) + EPS) * g


def _sigmoid(x):
    return 1.0 / (1.0 + jnp.exp(-x))


def _resident(shape):
    return pl.BlockSpec(shape, lambda *_: (0,) * len(shape), pipeline_mode=pl.Buffered(1))


def _params(*semantics):
    return pltpu.CompilerParams(dimension_semantics=semantics, vmem_limit_bytes=VMEM_LIMIT)


def _conv_layer_kernel(x_ref, ng_ref, w1_ref, b1_ref, dw_ref, dwb_ref, lng_ref, lnb_ref,
                       w2_ref, b2_ref, o_ref, ubuf, cbuf):
    tm, d = x_ref.shape[1], x_ref.shape[2]

    @pl.when(pl.program_id(1) == 0)
    def _():
        ubuf[0:HALO, :] = jnp.zeros((HALO, d), F32)

    x = x_ref[0]
    h = _rmsnorm(x, ng_ref[...]).astype(BF16)
    u = jnp.dot(h, w1_ref[...], preferred_element_type=F32) + b1_ref[...]
    ubuf[HALO:HALO + tm, :] = u[:, :d] * _sigmoid(u[:, d:])

    def strip(c, carry):
        col = pl.ds(pl.multiple_of(c * LANES, LANES), LANES)
        acc = jnp.broadcast_to(dwb_ref[:, col], (tm, LANES))
        for j in range(CONV_WIDTH):
            start = HALO - (CONV_WIDTH - 1) + j
            acc = acc + dw_ref[j:j + 1, col] * ubuf[start:start + tm, col]
        cbuf[:, col] = acc
        return carry

    lax.fori_loop(0, d // LANES, strip, 0)

    ubuf[0:HALO, :] = ubuf[tm:tm + HALO, :]

    c = cbuf[...]
    mu = jnp.mean(c, axis=-1, keepdims=True)
    cc = c - mu
    y = cc * lax.rsqrt(jnp.mean(cc * cc, axis=-1, keepdims=True) + EPS) * lng_ref[...] + lnb_ref[...]
    y = (y * _sigmoid(y)).astype(BF16)
    o_ref[0] = x + jnp.dot(y, w2_ref[...], preferred_element_type=F32) + b2_ref[...]


def _conv_layer(x, ng, w1, b1, dw, dwb, lng, lnb, w2, b2):
    b, s, d = x.shape
    tm = ROW_TILE
    tile = pl.BlockSpec((1, tm, d), lambda bi, si: (bi, si, 0))
    return pl.pallas_call(
        _conv_layer_kernel,
        grid=(b, s // tm),
        in_specs=[tile, _resident((1, d)), _resident((d, 2 * d)), _resident((1, 2 * d)),
                  _resident((CONV_WIDTH, d)), _resident((1, d)), _resident((1, d)),
                  _resident((1, d)), _resident((d, d)), _resident((1, d))],
        out_specs=tile,
        out_shape=jax.ShapeDtypeStruct(x.shape, F32),
        scratch_shapes=[pltpu.VMEM((HALO + tm, d), F32), pltpu.VMEM((tm, d), F32)],
        compiler_params=_params("arbitrary", "arbitrary"),
        name="conv_layer",
    )(x, ng, w1, b1, dw, dwb, lng, lnb, w2, b2)


def _ffn_kernel(x_ref, g_ref, w1_ref, w3_ref, w2_ref, fg_ref, o_ref, *, final_norm):
    x = x_ref[...]
    h = _rmsnorm(x, g_ref[...]).astype(BF16)
    h1 = jnp.dot(h, w1_ref[...], preferred_element_type=F32)
    h3 = jnp.dot(h, w3_ref[...], preferred_element_type=F32)
    act = ((h1 * _sigmoid(h1)) * h3).astype(BF16)
    y = x + jnp.dot(act, w2_ref[...], preferred_element_type=F32)
    if final_norm:
        y = _rmsnorm(y, fg_ref[...])
    o_ref[...] = y


def _ffn(x2, g, w1, w3, w2, fg, final_norm):
    m, d = x2.shape
    f = w1.shape[1]
    tm = ROW_TILE
    tile = pl.BlockSpec((tm, d), lambda i: (i, 0))
    return pl.pallas_call(
        functools.partial(_ffn_kernel, final_norm=final_norm),
        grid=(m // tm,),
        in_specs=[tile, _resident((1, d)), _resident((d, f)), _resident((d, f)),
                  _resident((f, d)), _resident((1, d))],
        out_specs=tile,
        out_shape=jax.ShapeDtypeStruct(x2.shape, F32),
        compiler_params=_params("arbitrary"),
        name="ffn",
    )(x2, g, w1, w3, w2, fg)


def _norm_proj_kernel(x_ref, g_ref, *refs, scale):
    n = len(refs) // 2
    h = _rmsnorm(x_ref[...], g_ref[...]).astype(BF16)
    for w_ref, o_ref in zip(refs[:n], refs[n:]):
        y = jnp.dot(h, w_ref[...], preferred_element_type=F32)
        if scale != 1.0:
            y = y * scale
        o_ref[...] = y.astype(BF16)


def _norm_proj(x2, g, ws, scale=1.0):
    m, d = x2.shape
    tm = ROW_TILE
    tile = pl.BlockSpec((tm, d), lambda i: (i, 0))
    outs = [pl.BlockSpec((tm, w.shape[1]), lambda i: (i, 0)) for w in ws]
    return pl.pallas_call(
        functools.partial(_norm_proj_kernel, scale=scale),
        grid=(m // tm,),
        in_specs=[tile, _resident((1, d))] + [_resident(w.shape) for w in ws],
        out_specs=outs,
        out_shape=[jax.ShapeDtypeStruct((m, w.shape[1]), BF16) for w in ws],
        compiler_params=_params("arbitrary"),
        name="norm_proj",
    )(x2, g, *ws)


def _attn_kernel(slopes_ref, q_ref, k_ref, v_ref, lq1_ref, lk1_ref, lq2_ref, lk2_ref, sg_ref,
                 o_ref, *, lambda_init):
    tq = q_ref.shape[1]
    tk = KV_TILE
    i = pl.program_id(2)
    neg_slope = -slopes_ref[pl.program_id(1)]

    q = q_ref[0]
    lane = lax.broadcasted_iota(jnp.int32, q.shape, 1)
    zero = jnp.zeros_like(q)
    qs = jnp.concatenate([jnp.where(lane < HEAD_DIM, q, zero),
                          jnp.where(lane >= HEAD_DIM, q, zero)], axis=0)

    row = lax.broadcasted_iota(jnp.int32, (2 * tq, tk), 0)
    row = jnp.where(row >= tq, row - tq, row)
    col = lax.broadcasted_iota(jnp.int32, (2 * tq, tk), 1)
    rel = (row - col).astype(F32)

    def scores(j):
        kb = k_ref[0, pl.ds(pl.multiple_of(j * tk, tk), tk), :]
        s = lax.dot_general(qs, kb, (((1,), (1,)), ((), ())), preferred_element_type=F32)
        dist = rel + ((i * tq - j * tk).astype(F32))
        return s + neg_slope * dist

    def update(j, s, carry):
        m, l, acc = carry
        vb = v_ref[0, pl.ds(pl.multiple_of(j * tk, tk), tk), :]
        m_new = jnp.maximum(m, jnp.max(s, axis=-1, keepdims=True))
        alpha = jnp.exp(m - m_new)
        p = jnp.exp(s - m_new)
        l = alpha * l + jnp.sum(p, axis=-1, keepdims=True)
        acc = alpha * acc + jnp.dot(p.astype(BF16), vb, preferred_element_type=F32)
        return m_new, l, acc

    init = (jnp.full((2 * tq, 1), MASK_VALUE, F32), jnp.zeros((2 * tq, 1), F32),
            jnp.zeros((2 * tq, V_DIM), F32))
    carry = lax.fori_loop(0, i, lambda j, c: update(j, scores(j), c), init)
    s_diag = jnp.where(rel >= 0, scores(i), MASK_VALUE)
    _, l, acc = update(i, s_diag, carry)

    lam = (jnp.exp(jnp.sum(lq1_ref[...] * lk1_ref[...], axis=-1, keepdims=True))
           - jnp.exp(jnp.sum(lq2_ref[...] * lk2_ref[...], axis=-1, keepdims=True))
           + lambda_init)
    o = acc / l
    o = o[:tq] - lam * o[tq:]
    o = _rmsnorm(o, sg_ref[...]) * (1.0 - lambda_init)
    o_ref[0] = o.astype(BF16)


def _attention(q, k, v, slopes, lq1, lk1, lq2, lk2, sg, lambda_init):
    b, s, _ = q.shape
    tq = Q_TILE
    assert Q_TILE == KV_TILE and s % tq == 0
    vec = _resident((1, HEAD_DIM))
    return pl.pallas_call(
        functools.partial(_attn_kernel, lambda_init=lambda_init),
        grid=(b, N_HEADS, s // tq),
        in_specs=[pl.BlockSpec(memory_space=pltpu.SMEM),
                  pl.BlockSpec((1, tq, V_DIM), lambda bi, hi, qi: (bi, qi, hi)),
                  pl.BlockSpec((1, s, V_DIM), lambda bi, hi, qi: (bi, 0, hi)),
                  pl.BlockSpec((1, s, V_DIM), lambda bi, hi, qi: (bi, 0, hi)),
                  vec, vec, vec, vec, _resident((1, V_DIM))],
        out_specs=pl.BlockSpec((1, tq, V_DIM), lambda bi, hi, qi: (bi, qi, hi)),
        out_shape=jax.ShapeDtypeStruct(q.shape, BF16),
        compiler_params=_params("arbitrary", "arbitrary", "arbitrary"),
        name="diff_attention",
    )(slopes, q, k, v, lq1, lk1, lq2, lk2, sg)


def _proj_residual_kernel(a_ref, w_ref, x_ref, o_ref):
    o_ref[...] = x_ref[...] + jnp.dot(a_ref[...], w_ref[...], preferred_element_type=F32)


def _proj_residual(a2, w, x2):
    m, d = x2.shape
    tm = ROW_TILE
    return pl.pallas_call(
        _proj_residual_kernel,
        grid=(m // tm,),
        in_specs=[pl.BlockSpec((tm, a2.shape[1]), lambda i: (i, 0)), _resident(w.shape),
                  pl.BlockSpec((tm, d), lambda i: (i, 0))],
        out_specs=pl.BlockSpec((tm, d), lambda i: (i, 0)),
        out_shape=jax.ShapeDtypeStruct(x2.shape, F32),
        compiler_params=_params("arbitrary"),
        name="proj_residual",
    )(a2, w, x2)


def kernel(x, conv_norm_g, pw1_w, pw1_b, dw_w, dw_b, conv_ln_g, conv_ln_b, pw2_w, pw2_b,
           kv_norm_g, w_k, w_v, attn_norm_g, w_q, lambda_q1, lambda_k1, lambda_q2, lambda_k2,
           subln_g, w_o, ffn_norm_g, ffn_w1, ffn_w3, ffn_w2, final_norm_g):
    b, s, d = x.shape
    depth = ffn_w1.shape[0]
    n_conv = pw1_w.shape[0]
    row = lambda a: a.reshape(1, -1)
    slopes = jnp.asarray((2.0 ** (-8.0 * np.arange(1, N_HEADS + 1) / N_HEADS)).astype(np.float32))
    scale = HEAD_DIM ** -0.5

    k = v = None
    for layer in range(depth):
        if layer < n_conv:
            x = _conv_layer(x, row(conv_norm_g[layer]), pw1_w[layer].astype(BF16), row(pw1_b[layer]),
                            dw_w[layer], row(dw_b[layer]), row(conv_ln_g[layer]),
                            row(conv_ln_b[layer]), pw2_w[layer].astype(BF16), row(pw2_b[layer]))
        else:
            j = layer - n_conv
            lambda_init = 0.8 - 0.6 * math.exp(-0.3 * layer)
            (q,) = _norm_proj(x.reshape(b * s, d), row(attn_norm_g[j]), [w_q[j].astype(BF16)], scale)
            o = _attention(q.reshape(b, s, -1), k, v, slopes, row(lambda_q1[j]), row(lambda_k1[j]),
                           row(lambda_q2[j]), row(lambda_k2[j]), row(subln_g[j]), lambda_init)
            x = _proj_residual(o.reshape(b * s, -1), w_o[j].astype(BF16),
                               x.reshape(b * s, d)).reshape(b, s, d)
        x = _ffn(x.reshape(b * s, d), row(ffn_norm_g[layer]), ffn_w1[layer].astype(BF16),
                 ffn_w3[layer].astype(BF16), ffn_w2[layer].astype(BF16), row(final_norm_g),
                 final_norm=(layer == depth - 1)).reshape(b, s, d)
        if layer == n_conv - 1:
            k, v = _norm_proj(x.reshape(b * s, d), row(kv_norm_g),
                              [w_k.astype(BF16), w_v.astype(BF16)])
            k = k.reshape(b, s, -1)
            v = v.reshape(b, s, -1)
    return x
```

```python
import functools
import math

import jax
import jax.numpy as jnp
import numpy as np
from jax import lax
from jax.experimental import pallas as pl
from jax.experimental.pallas import tpu as pltpu

F32 = jnp.float32
BF16 = jnp.bfloat16

EPS = 1e-6
CONV_WIDTH = 31
N_HEADS = 8
HEAD_DIM = 64
V_DIM = 2 * HEAD_DIM
HALO = 32
LANES = 128
VMEM_LIMIT = 56 * 1024 * 1024

ROW_TILE = 256
ATTN_TILE = 256
MASK_VALUE = -1e30
LOG2E = math.log2(math.e)


def _rmsnorm(x, g):
    return x * lax.rsqrt(jnp.mean(x * x, axis=-1, keepdims=True) + EPS) * g


def _sigmoid(x):
    return 1.0 / (1.0 + jnp.exp(-x))


def _col_reduce(x, op, groups=8):
    rows, cols = x.shape
    return op(op(x.reshape(groups, rows // groups, cols), axis=0), axis=0, keepdims=True)


def _resident(shape):
    return pl.BlockSpec(shape, lambda *_: (0,) * len(shape), pipeline_mode=pl.Buffered(1))


def _params(*semantics):
    return pltpu.CompilerParams(dimension_semantics=semantics, vmem_limit_bytes=VMEM_LIMIT)


def _conv_layer_kernel(x_ref, ng_ref, w1_ref, b1_ref, dw_ref, dwb_ref, lng_ref, lnb_ref,
                       w2_ref, b2_ref, o_ref, ubuf, cbuf):
    tm, d = x_ref.shape[1], x_ref.shape[2]

    @pl.when(pl.program_id(1) == 0)
    def _():
        ubuf[0:HALO, :] = jnp.zeros((HALO, d), F32)

    x = x_ref[0]
    h = _rmsnorm(x, ng_ref[...]).astype(BF16)
    u = jnp.dot(h, w1_ref[...], preferred_element_type=F32) + b1_ref[...]
    ubuf[HALO:HALO + tm, :] = u[:, :d] * _sigmoid(u[:, d:])

    def strip(c, carry):
        col = pl.ds(pl.multiple_of(c * LANES, LANES), LANES)
        acc = jnp.broadcast_to(dwb_ref[:, col], (tm, LANES))
        for j in range(CONV_WIDTH):
            start = HALO - (CONV_WIDTH - 1) + j
            acc = acc + dw_ref[j:j + 1, col] * ubuf[start:start + tm, col]
        cbuf[:, col] = acc
        return carry

    lax.fori_loop(0, d // LANES, strip, 0)

    ubuf[0:HALO, :] = ubuf[tm:tm + HALO, :]

    c = cbuf[...]
    mu = jnp.mean(c, axis=-1, keepdims=True)
    cc = c - mu
    y = cc * lax.rsqrt(jnp.mean(cc * cc, axis=-1, keepdims=True) + EPS) * lng_ref[...] + lnb_ref[...]
    y = (y * _sigmoid(y)).astype(BF16)
    o_ref[0] = x + jnp.dot(y, w2_ref[...], preferred_element_type=F32) + b2_ref[...]


def _conv_layer(x, ng, w1, b1, dw, dwb, lng, lnb, w2, b2):
    b, s, d = x.shape
    tm = ROW_TILE
    tile = pl.BlockSpec((1, tm, d), lambda bi, si: (bi, si, 0))
    return pl.pallas_call(
        _conv_layer_kernel,
        grid=(b, s // tm),
        in_specs=[tile, _resident((1, d)), _resident((d, 2 * d)), _resident((1, 2 * d)),
                  _resident((CONV_WIDTH, d)), _resident((1, d)), _resident((1, d)),
                  _resident((1, d)), _resident((d, d)), _resident((1, d))],
        out_specs=tile,
        out_shape=jax.ShapeDtypeStruct(x.shape, F32),
        scratch_shapes=[pltpu.VMEM((HALO + tm, d), F32), pltpu.VMEM((tm, d), F32)],
        compiler_params=_params("arbitrary", "arbitrary"),
        name="conv_layer",
    )(x, ng, w1, b1, dw, dwb, lng, lnb, w2, b2)


def _ffn_kernel(x_ref, g_ref, w1_ref, w3_ref, w2_ref, fg_ref, o_ref, *, final_norm):
    x = x_ref[...]
    h = _rmsnorm(x, g_ref[...]).astype(BF16)
    h1 = jnp.dot(h, w1_ref[...], preferred_element_type=F32)
    h3 = jnp.dot(h, w3_ref[...], preferred_element_type=F32)
    act = ((h1 * _sigmoid(h1)) * h3).astype(BF16)
    y = x + jnp.dot(act, w2_ref[...], preferred_element_type=F32)
    if final_norm:
        y = _rmsnorm(y, fg_ref[...])
    o_ref[...] = y


def _ffn(x2, g, w1, w3, w2, fg, final_norm):
    m, d = x2.shape
    f = w1.shape[1]
    tm = ROW_TILE
    tile = pl.BlockSpec((tm, d), lambda i: (i, 0))
    return pl.pallas_call(
        functools.partial(_ffn_kernel, final_norm=final_norm),
        grid=(m // tm,),
        in_specs=[tile, _resident((1, d)), _resident((d, f)), _resident((d, f)),
                  _resident((f, d)), _resident((1, d))],
        out_specs=tile,
        out_shape=jax.ShapeDtypeStruct(x2.shape, F32),
        compiler_params=_params("arbitrary"),
        name="ffn",
    )(x2, g, w1, w3, w2, fg)


def _norm_proj_kernel(x_ref, g_ref, *refs, transposed, scale):
    n = len(transposed)
    h = _rmsnorm(x_ref[0], g_ref[...]).astype(BF16)
    for w_ref, o_ref, t in zip(refs[:n], refs[n:], transposed):
        if t:
            y = lax.dot_general(w_ref[...], h, (((1,), (1,)), ((), ())), preferred_element_type=F32)
        else:
            y = jnp.dot(h, w_ref[...], preferred_element_type=F32)
        if scale != 1.0:
            y = y * scale
        o_ref[0] = y.astype(BF16)


def _norm_proj(x, g, ws, transposed, scale=1.0):
    b, s, d = x.shape
    tm = ROW_TILE
    out_specs, out_shapes = [], []
    for w, t in zip(ws, transposed):
        if t:
            out_specs.append(pl.BlockSpec((1, w.shape[0], tm), lambda bi, si: (bi, 0, si)))
            out_shapes.append(jax.ShapeDtypeStruct((b, w.shape[0], s), BF16))
        else:
            out_specs.append(pl.BlockSpec((1, tm, w.shape[1]), lambda bi, si: (bi, si, 0)))
            out_shapes.append(jax.ShapeDtypeStruct((b, s, w.shape[1]), BF16))
    return pl.pallas_call(
        functools.partial(_norm_proj_kernel, transposed=tuple(transposed), scale=scale),
        grid=(b, s // tm),
        in_specs=[pl.BlockSpec((1, tm, d), lambda bi, si: (bi, si, 0)), _resident((1, d))]
        + [_resident(w.shape) for w in ws],
        out_specs=out_specs,
        out_shape=out_shapes,
        compiler_params=_params("arbitrary", "arbitrary"),
        name="norm_proj",
    )(x, g, *ws)


def _attn_kernel(slopes_ref, itab_ref, jtab_ref, qt_ref, k_ref, vt_ref, lq1_ref, lk1_ref,
                 lq2_ref, lk2_ref, sg_ref, o_ref, bias_scr, qs_scr, s_scr, p_scr, acc_scr,
                 *, lambda_init, n_pairs):
    t_ = ATTN_TILE
    neg_slope2 = -slopes_ref[pl.program_id(1)] * LOG2E

    key = lax.broadcasted_iota(jnp.int32, (t_, t_), 0)
    qry = lax.broadcasted_iota(jnp.int32, (t_, t_), 1)
    bias_rel = neg_slope2 * (qry - key).astype(F32)
    bias_scr[0] = bias_rel
    bias_scr[1] = jnp.where(qry >= key, bias_rel, MASK_VALUE)

    feat = lax.broadcasted_iota(jnp.int32, (V_DIM, t_), 0)
    lam = (jnp.exp(jnp.sum(lq1_ref[...] * lk1_ref[...], axis=-1, keepdims=True))
           - jnp.exp(jnp.sum(lq2_ref[...] * lk2_ref[...], axis=-1, keepdims=True))
           + lambda_init)

    def tile(idx):
        return pl.ds(pl.multiple_of(idx * t_, t_), t_)

    for i in range(qt_ref.shape[2] // t_):
        qt = qt_ref[0, :, i * t_:(i + 1) * t_]
        zero = jnp.zeros_like(qt)
        qs_scr[:, 2 * i * t_:(2 * i + 1) * t_] = jnp.where(feat < HEAD_DIM, qt, zero)
        qs_scr[:, (2 * i + 1) * t_:(2 * i + 2) * t_] = jnp.where(feat >= HEAD_DIM, qt, zero)

    def qk(t):
        qs = qs_scr[:, pl.ds(pl.multiple_of(itab_ref[t] * (2 * t_), 2 * t_), 2 * t_)]
        return jnp.dot(k_ref[0, tile(jtab_ref[t]), :], qs, preferred_element_type=F32)

    def pv(t, alpha):
        acc_scr[...] = alpha * acc_scr[...] + jnp.dot(vt_ref[0, :, tile(jtab_ref[t])], p_scr[...],
                                                      preferred_element_type=F32)

    def finish(i, l):
        on = acc_scr[...] * (1.0 / l)
        ot = on[:, :t_] - lam * on[:, t_:]
        o = _rmsnorm(ot.T, sg_ref[...]) * (1.0 - lambda_init)
        o_ref[0, tile(i), :] = o.astype(BF16)

    def step(t, carry):
        m, l, alpha_prev = carry
        tp = jnp.maximum(t - 1, 0)
        pv(tp, alpha_prev)

        i, j = itab_ref[t], jtab_ref[t]
        b = bias_scr[(i == j).astype(jnp.int32)]
        z = s_scr[...] + jnp.concatenate([b, b], axis=1)
        c = neg_slope2 * ((i - j) * t_).astype(F32)
        m_old = jnp.where(j == 0, MASK_VALUE, m)
        m_new = jnp.maximum(m_old, _col_reduce(z, jnp.max) + c)
        alpha = jnp.exp2(m_old - m_new)
        p = jnp.exp2(z - (m_new - c))
        l_new = alpha * l + _col_reduce(p, jnp.sum)
        p_scr[...] = p.astype(BF16)

        s_scr[...] = qk(t + 1)

        @pl.when((t > 0) & (itab_ref[tp] == jtab_ref[tp]))
        def _():
            finish(itab_ref[tp], l)

        return m_new, l_new, alpha

    row = jnp.zeros((1, 2 * t_), F32)
    acc_scr[...] = jnp.zeros(acc_scr.shape, F32)
    p_scr[...] = jnp.zeros(p_scr.shape, BF16)
    s_scr[...] = qk(0)
    _, l, alpha_prev = lax.fori_loop(0, n_pairs, step, (row, row, row))
    last = n_pairs - 1
    pv(last, alpha_prev)
    finish(itab_ref[last], l)


def _attention(qt, k, vt, slopes, lq1, lk1, lq2, lk2, sg, lambda_init):
    b, s, _ = k.shape
    t_ = ATTN_TILE
    nq = s // t_
    pairs = [(i, j) for i in range(nq) for j in range(i + 1)]
    itab = jnp.asarray([i for i, _ in pairs] + [0], jnp.int32)
    jtab = jnp.asarray([j for _, j in pairs] + [0], jnp.int32)
    vec = _resident((1, HEAD_DIM))
    smem = pl.BlockSpec(memory_space=pltpu.SMEM)
    return pl.pallas_call(
        functools.partial(_attn_kernel, lambda_init=lambda_init, n_pairs=len(pairs)),
        grid=(b, N_HEADS),
        in_specs=[smem, smem, smem,
                  pl.BlockSpec((1, V_DIM, s), lambda bi, hi: (bi, hi, 0)),
                  pl.BlockSpec((1, s, V_DIM), lambda bi, hi: (bi, 0, hi)),
                  pl.BlockSpec((1, V_DIM, s), lambda bi, hi: (bi, hi, 0)),
                  vec, vec, vec, vec, _resident((1, V_DIM))],
        out_specs=pl.BlockSpec((1, s, V_DIM), lambda bi, hi: (bi, 0, hi)),
        out_shape=jax.ShapeDtypeStruct(k.shape, BF16),
        scratch_shapes=[pltpu.VMEM((2, t_, t_), F32), pltpu.VMEM((V_DIM, 2 * s), BF16),
                        pltpu.VMEM((t_, 2 * t_), F32),
                        pltpu.VMEM((t_, 2 * t_), BF16), pltpu.VMEM((V_DIM, 2 * t_), F32)],
        compiler_params=_params("arbitrary", "arbitrary"),
        name="diff_attention",
    )(slopes, itab, jtab, qt, k, vt, lq1, lk1, lq2, lk2, sg)


def _proj_residual_kernel(a_ref, w_ref, x_ref, o_ref):
    o_ref[...] = x_ref[...] + jnp.dot(a_ref[...], w_ref[...], preferred_element_type=F32)


def _proj_residual(a2, w, x2):
    m, d = x2.shape
    tm = ROW_TILE
    return pl.pallas_call(
        _proj_residual_kernel,
        grid=(m // tm,),
        in_specs=[pl.BlockSpec((tm, a2.shape[1]), lambda i: (i, 0)), _resident(w.shape),
                  pl.BlockSpec((tm, d), lambda i: (i, 0))],
        out_specs=pl.BlockSpec((tm, d), lambda i: (i, 0)),
        out_shape=jax.ShapeDtypeStruct(x2.shape, F32),
        compiler_params=_params("arbitrary"),
        name="proj_residual",
    )(a2, w, x2)


def kernel(x, conv_norm_g, pw1_w, pw1_b, dw_w, dw_b, conv_ln_g, conv_ln_b, pw2_w, pw2_b,
           kv_norm_g, w_k, w_v, attn_norm_g, w_q, lambda_q1, lambda_k1, lambda_q2, lambda_k2,
           subln_g, w_o, ffn_norm_g, ffn_w1, ffn_w3, ffn_w2, final_norm_g):
    b, s, d = x.shape
    depth = ffn_w1.shape[0]
    n_conv = pw1_w.shape[0]
    row = lambda a: a.reshape(1, -1)
    slopes = jnp.asarray((2.0 ** (-8.0 * np.arange(1, N_HEADS + 1) / N_HEADS)).astype(np.float32))
    scale = HEAD_DIM ** -0.5 * LOG2E

    k = vt = None
    for layer in range(depth):
        if layer < n_conv:
            x = _conv_layer(x, row(conv_norm_g[layer]), pw1_w[layer].astype(BF16), row(pw1_b[layer]),
                            dw_w[layer], row(dw_b[layer]), row(conv_ln_g[layer]),
                            row(conv_ln_b[layer]), pw2_w[layer].astype(BF16), row(pw2_b[layer]))
        else:
            j = layer - n_conv
            lambda_init = 0.8 - 0.6 * math.exp(-0.3 * layer)
            (qt,) = _norm_proj(x, row(attn_norm_g[j]), [w_q[j].T.astype(BF16)], [True], scale)
            o = _attention(qt, k, vt, slopes, row(lambda_q1[j]), row(lambda_k1[j]),
                           row(lambda_q2[j]), row(lambda_k2[j]), row(subln_g[j]), lambda_init)
            x = _proj_residual(o.reshape(b * s, -1), w_o[j].astype(BF16),
                               x.reshape(b * s, d)).reshape(b, s, d)
        x = _ffn(x.reshape(b * s, d), row(ffn_norm_g[layer]), ffn_w1[layer].astype(BF16),
                 ffn_w3[layer].astype(BF16), ffn_w2[layer].astype(BF16), row(final_norm_g),
                 final_norm=(layer == depth - 1)).reshape(b, s, d)
        if layer == n_conv - 1:
            k, vt = _norm_proj(x, row(kv_norm_g), [w_k.astype(BF16), w_v.T.astype(BF16)],
                               [False, True])
    return x
```

```python
import functools
import math

import jax
import jax.numpy as jnp
import numpy as np
from jax import lax
from jax.experimental import pallas as pl
from jax.experimental.pallas import tpu as pltpu

F32 = jnp.float32
BF16 = jnp.bfloat16

EPS = 1e-6
CONV_WIDTH = 31
N_HEADS = 8
HEAD_DIM = 64
V_DIM = 2 * HEAD_DIM
HALO = 32
LANES = 128
SUBLANES = 8
VMEM_LIMIT = 56 * 1024 * 1024

ROW_TILE = 256
ATTN_TILE = 256
ATTN_UNROLL = 6
MASK_VALUE = -1e30
LOG2E = math.log2(math.e)


def _rmsnorm(x, g):
    return x * lax.rsqrt(jnp.mean(x * x, axis=-1, keepdims=True) + EPS) * g


def _sigmoid(x):
    return 1.0 / (1.0 + jnp.exp(-x))


def _col_reduce(x, op, groups=8):
    rows, cols = x.shape
    return op(op(x.reshape(groups, rows // groups, cols), axis=0), axis=0, keepdims=True)


def _resident(shape):
    return pl.BlockSpec(shape, lambda *_: (0,) * len(shape), pipeline_mode=pl.Buffered(1))


def _params(*semantics):
    return pltpu.CompilerParams(dimension_semantics=semantics, vmem_limit_bytes=VMEM_LIMIT)


def _conv_layer_kernel(x_ref, ng_ref, w1_ref, b1_ref, dw_ref, dwb_ref, lng_ref, lnb_ref,
                       w2_ref, b2_ref, o_ref, ubuf, cbuf):
    tm, d = x_ref.shape[1], x_ref.shape[2]

    @pl.when(pl.program_id(1) == 0)
    def _():
        ubuf[0:HALO, :] = jnp.zeros((HALO, d), F32)

    x = x_ref[0]
    h = _rmsnorm(x, ng_ref[...]).astype(BF16)
    u = jnp.dot(h, w1_ref[...], preferred_element_type=F32) + b1_ref[...]
    ubuf[HALO:HALO + tm, :] = u[:, :d] * _sigmoid(u[:, d:])

    first = HALO - (CONV_WIDTH - 1)

    def strip(c, carry):
        col = pl.ds(pl.multiple_of(c * LANES, LANES), LANES)
        acc = jnp.broadcast_to(dwb_ref[:, col], (tm, LANES))
        for r in range(SUBLANES):
            rows = tm if r == 0 else tm + SUBLANES
            v = None
            for k in range(first + (r - first) % SUBLANES, first + CONV_WIDTH, SUBLANES):
                term = dw_ref[k - first:k - first + 1, col] * ubuf[k - r:k - r + rows, col]
                v = term if v is None else v + term
            acc = acc + (v if r == 0 else pltpu.roll(v, rows - r, axis=0)[:tm])
        cbuf[:, col] = acc
        return carry

    lax.fori_loop(0, d // LANES, strip, 0)

    ubuf[0:HALO, :] = ubuf[tm:tm + HALO, :]

    c = cbuf[...]
    mu = jnp.mean(c, axis=-1, keepdims=True)
    cc = c - mu
    y = cc * lax.rsqrt(jnp.mean(cc * cc, axis=-1, keepdims=True) + EPS) * lng_ref[...] + lnb_ref[...]
    y = (y * _sigmoid(y)).astype(BF16)
    o_ref[0] = x + jnp.dot(y, w2_ref[...], preferred_element_type=F32) + b2_ref[...]


def _conv_layer(x, ng, w1, b1, dw, dwb, lng, lnb, w2, b2):
    b, s, d = x.shape
    tm = ROW_TILE
    tile = pl.BlockSpec((1, tm, d), lambda bi, si: (bi, si, 0))
    return pl.pallas_call(
        _conv_layer_kernel,
        grid=(b, s // tm),
        in_specs=[tile, _resident((1, d)), _resident((d, 2 * d)), _resident((1, 2 * d)),
                  _resident((CONV_WIDTH, d)), _resident((1, d)), _resident((1, d)),
                  _resident((1, d)), _resident((d, d)), _resident((1, d))],
        out_specs=tile,
        out_shape=jax.ShapeDtypeStruct(x.shape, F32),
        scratch_shapes=[pltpu.VMEM((HALO + tm, d), F32), pltpu.VMEM((tm, d), F32)],
        compiler_params=_params("arbitrary", "arbitrary"),
        name="conv_layer",
    )(x, ng, w1, b1, dw, dwb, lng, lnb, w2, b2)


def _ffn_kernel(*refs, has_attn, final_norm, projs):
    refs = iter(refs)
    x_ref = next(refs)
    if has_attn:
        a_ref, wo_ref = next(refs), next(refs)
    g_ref, w1_ref, w3_ref, w2_ref = next(refs), next(refs), next(refs), next(refs)
    if final_norm:
        fg_ref = next(refs)
    proj_in = [(next(refs), next(refs)) for _ in projs]
    o_ref = next(refs)
    proj_out = [next(refs) for _ in projs]

    x = x_ref[0]
    if has_attn:
        x = x + jnp.dot(a_ref[0], wo_ref[...], preferred_element_type=F32)
    h = _rmsnorm(x, g_ref[...]).astype(BF16)
    h1 = jnp.dot(h, w1_ref[...], preferred_element_type=F32)
    h3 = jnp.dot(h, w3_ref[...], preferred_element_type=F32)
    act = ((h1 * _sigmoid(h1)) * h3).astype(BF16)
    y = x + jnp.dot(act, w2_ref[...], preferred_element_type=F32)
    if final_norm:
        y = _rmsnorm(y, fg_ref[...])
    o_ref[0] = y

    if projs:
        yn = y * lax.rsqrt(jnp.mean(y * y, axis=-1, keepdims=True) + EPS)
    for (pg_ref, pw_ref), po_ref, (transposed, scale) in zip(proj_in, proj_out, projs):
        hp = (yn * pg_ref[...]).astype(BF16)
        if transposed:
            p = lax.dot_general(pw_ref[...], hp, (((1,), (1,)), ((), ())), preferred_element_type=F32)
        else:
            p = jnp.dot(hp, pw_ref[...], preferred_element_type=F32)
        if scale != 1.0:
            p = p * scale
        po_ref[0] = p.astype(BF16)


def _ffn(x, g, w1, w3, w2, *, attn=None, final_g=None, projs=()):
    b, s, d = x.shape
    f = w1.shape[1]
    tm = ROW_TILE
    tile = pl.BlockSpec((1, tm, d), lambda bi, si: (bi, si, 0))
    args, in_specs = [x], [tile]
    if attn is not None:
        args += list(attn)
        in_specs += [tile, _resident(attn[1].shape)]
    args += [g, w1, w3, w2]
    in_specs += [_resident((1, d)), _resident((d, f)), _resident((d, f)), _resident((f, d))]
    if final_g is not None:
        args.append(final_g)
        in_specs.append(_resident((1, d)))
    out_specs, out_shapes = [tile], [jax.ShapeDtypeStruct(x.shape, F32)]
    for pg, pw, transposed, _ in projs:
        args += [pg, pw]
        in_specs += [_resident((1, d)), _resident(pw.shape)]
        if transposed:
            out_specs.append(pl.BlockSpec((1, pw.shape[0], tm), lambda bi, si: (bi, 0, si)))
            out_shapes.append(jax.ShapeDtypeStruct((b, pw.shape[0], s), BF16))
        else:
            out_specs.append(pl.BlockSpec((1, tm, pw.shape[1]), lambda bi, si: (bi, si, 0)))
            out_shapes.append(jax.ShapeDtypeStruct((b, s, pw.shape[1]), BF16))
    return pl.pallas_call(
        functools.partial(_ffn_kernel, has_attn=attn is not None, final_norm=final_g is not None,
                          projs=tuple((t, sc) for _, _, t, sc in projs)),
        grid=(b, s // tm),
        in_specs=in_specs,
        out_specs=out_specs,
        out_shape=out_shapes,
        compiler_params=_params("arbitrary", "arbitrary"),
        name="ffn",
    )(*args)


def _attn_kernel(slopes_ref, itab_ref, jtab_ref, qt_ref, k_ref, vt_ref, lq1_ref, lk1_ref,
                 lq2_ref, lk2_ref, sg_ref, o_ref, bias_scr, qs_scr, s_scr, p_scr, acc_scr, l_scr,
                 *, lambda_init, n_pairs):
    t_ = ATTN_TILE
    neg_slope2 = -slopes_ref[pl.program_id(1)] * LOG2E

    key = lax.broadcasted_iota(jnp.int32, (t_, t_), 0)
    qry = lax.broadcasted_iota(jnp.int32, (t_, t_), 1)
    bias_rel = neg_slope2 * (qry - key).astype(F32)
    bias_scr[0] = bias_rel
    bias_scr[1] = jnp.where(qry >= key, bias_rel, MASK_VALUE)

    feat = lax.broadcasted_iota(jnp.int32, (V_DIM, t_), 0)
    lam = (jnp.exp(jnp.sum(lq1_ref[...] * lk1_ref[...], axis=-1, keepdims=True))
           - jnp.exp(jnp.sum(lq2_ref[...] * lk2_ref[...], axis=-1, keepdims=True))
           + lambda_init)

    def tile(idx):
        return pl.ds(pl.multiple_of(idx * t_, t_), t_)

    for i in range(qt_ref.shape[2] // t_):
        qt = qt_ref[0, :, i * t_:(i + 1) * t_]
        zero = jnp.zeros_like(qt)
        qs_scr[:, 2 * i * t_:(2 * i + 1) * t_] = jnp.where(feat < HEAD_DIM, qt, zero)
        qs_scr[:, (2 * i + 1) * t_:(2 * i + 2) * t_] = jnp.where(feat >= HEAD_DIM, qt, zero)

    def qk(t):
        qs = qs_scr[:, pl.ds(pl.multiple_of(itab_ref[t] * (2 * t_), 2 * t_), 2 * t_)]
        return jnp.dot(k_ref[0, tile(jtab_ref[t]), :], qs, preferred_element_type=F32)

    def pv(t, slot, alpha):
        i = itab_ref[t]
        acc_scr[i] = alpha * acc_scr[i] + jnp.dot(vt_ref[0, :, tile(jtab_ref[t])], p_scr[slot],
                                                  preferred_element_type=F32)

    def softmax(t, slot, m, l):
        i, j = itab_ref[t], jtab_ref[t]
        b = bias_scr[(i == j).astype(jnp.int32)]
        z = s_scr[slot] + jnp.concatenate([b, b], axis=1)
        c = neg_slope2 * ((i - j) * t_).astype(F32)
        m_old = jnp.where(j == 0, MASK_VALUE, m)
        m_new = jnp.maximum(m_old, _col_reduce(z, jnp.max) + c)
        alpha = jnp.exp2(m_old - m_new)
        p = jnp.exp2(z - (m_new - c))
        l_new = alpha * l + _col_reduce(p, jnp.sum)
        p_scr[slot] = p.astype(BF16)
        l_scr[pl.ds(i, 1), :] = l_new
        return m_new, l_new, alpha

    def step(g, carry):
        m, l, alpha = carry
        for u in range(ATTN_UNROLL):
            t = g * ATTN_UNROLL + u
            s_scr[(u + 1) % 2] = qk(t + 1)
            pv(jnp.maximum(t - 1, 0), (u + 1) % 2, alpha)
            m, l, alpha = softmax(t, u % 2, m, l)
        return m, l, alpha

    row = jnp.zeros((1, 2 * t_), F32)
    acc_scr[...] = jnp.zeros(acc_scr.shape, F32)
    p_scr[1] = jnp.zeros(p_scr.shape[1:], BF16)
    s_scr[0] = qk(0)
    assert n_pairs % ATTN_UNROLL == 0 and ATTN_UNROLL % 2 == 0
    _, _, alpha = lax.fori_loop(0, n_pairs // ATTN_UNROLL, step, (row, row, row))
    pv(n_pairs - 1, (n_pairs - 1) % 2, alpha)

    def finish(i, carry):
        on = acc_scr[i] * (1.0 / l_scr[pl.ds(i, 1), :])
        ot = on[:, :t_] - lam * on[:, t_:]
        o = _rmsnorm(ot.T, sg_ref[...]) * (1.0 - lambda_init)
        o_ref[0, tile(i), :] = o.astype(BF16)
        return carry

    lax.fori_loop(0, acc_scr.shape[0], finish, 0)


def _attention(qt, k, vt, slopes, lq1, lk1, lq2, lk2, sg, lambda_init):
    b, s, _ = k.shape
    t_ = ATTN_TILE
    nq = s // t_
    pairs = [(i, j) for i in range(nq) for j in range(i + 1)]
    itab = jnp.asarray([i for i, _ in pairs] + [0], jnp.int32)
    jtab = jnp.asarray([j for _, j in pairs] + [0], jnp.int32)
    vec = _resident((1, HEAD_DIM))
    smem = pl.BlockSpec(memory_space=pltpu.SMEM)
    return pl.pallas_call(
        functools.partial(_attn_kernel, lambda_init=lambda_init, n_pairs=len(pairs)),
        grid=(b, N_HEADS),
        in_specs=[smem, smem, smem,
                  pl.BlockSpec((1, V_DIM, s), lambda bi, hi: (bi, hi, 0)),
                  pl.BlockSpec((1, s, V_DIM), lambda bi, hi: (bi, 0, hi)),
                  pl.BlockSpec((1, V_DIM, s), lambda bi, hi: (bi, hi, 0)),
                  vec, vec, vec, vec, _resident((1, V_DIM))],
        out_specs=pl.BlockSpec((1, s, V_DIM), lambda bi, hi: (bi, 0, hi)),
        out_shape=jax.ShapeDtypeStruct(k.shape, BF16),
        scratch_shapes=[pltpu.VMEM((2, t_, t_), F32), pltpu.VMEM((V_DIM, 2 * s), BF16),
                        pltpu.VMEM((2, t_, 2 * t_), F32), pltpu.VMEM((2, t_, 2 * t_), BF16),
                        pltpu.VMEM((nq, V_DIM, 2 * t_), F32), pltpu.VMEM((nq, 2 * t_), F32)],
        compiler_params=_params("arbitrary", "arbitrary"),
        name="diff_attention",
    )(slopes, itab, jtab, qt, k, vt, lq1, lk1, lq2, lk2, sg)


def kernel(x, conv_norm_g, pw1_w, pw1_b, dw_w, dw_b, conv_ln_g, conv_ln_b, pw2_w, pw2_b,
           kv_norm_g, w_k, w_v, attn_norm_g, w_q, lambda_q1, lambda_k1, lambda_q2, lambda_k2,
           subln_g, w_o, ffn_norm_g, ffn_w1, ffn_w3, ffn_w2, final_norm_g):
    b, s, d = x.shape
    depth = ffn_w1.shape[0]
    n_conv = pw1_w.shape[0]
    row = lambda a: a.reshape(1, -1)
    slopes = jnp.asarray((2.0 ** (-8.0 * np.arange(1, N_HEADS + 1) / N_HEADS)).astype(np.float32))
    scale = HEAD_DIM ** -0.5 * LOG2E

    assert 1 <= n_conv < depth
    k = vt = qt = None
    for layer in range(depth):
        attn = None
        if layer < n_conv:
            x = _conv_layer(x, row(conv_norm_g[layer]), pw1_w[layer].astype(BF16), row(pw1_b[layer]),
                            dw_w[layer], row(dw_b[layer]), row(conv_ln_g[layer]),
                            row(conv_ln_b[layer]), pw2_w[layer].astype(BF16), row(pw2_b[layer]))
        else:
            j = layer - n_conv
            lambda_init = 0.8 - 0.6 * math.exp(-0.3 * layer)
            o = _attention(qt, k, vt, slopes, row(lambda_q1[j]), row(lambda_k1[j]),
                           row(lambda_q2[j]), row(lambda_k2[j]), row(subln_g[j]), lambda_init)
            attn = (o, w_o[j].astype(BF16))
        projs = []
        if layer == n_conv - 1:
            projs += [(row(kv_norm_g), w_k.astype(BF16), False, 1.0),
                      (row(kv_norm_g), w_v.T.astype(BF16), True, 1.0)]
        if n_conv - 1 <= layer < depth - 1:
            jn = layer + 1 - n_conv
            projs.append((row(attn_norm_g[jn]), w_q[jn].T.astype(BF16), True, scale))
        outs = _ffn(x, row(ffn_norm_g[layer]), ffn_w1[layer].astype(BF16),
                    ffn_w3[layer].astype(BF16), ffn_w2[layer].astype(BF16), attn=attn,
                    final_g=row(final_norm_g) if layer == depth - 1 else None, projs=projs)
        x = outs[0]
        if layer == n_conv - 1:
            k, vt = outs[1], outs[2]
        if projs:
            qt = outs[-1]
    return x
```

```python
import functools
import math

import jax
import jax.numpy as jnp
import numpy as np
from jax import lax
from jax.experimental import pallas as pl
from jax.experimental.pallas import tpu as pltpu

F32 = jnp.float32
BF16 = jnp.bfloat16

EPS = 1e-6
CONV_WIDTH = 31
N_HEADS = 8
HEAD_DIM = 64
V_DIM = 2 * HEAD_DIM
HALO = 32
LANES = 128
SUBLANES = 8
VMEM_LIMIT = 56 * 1024 * 1024

ROW_TILE = 256
FFN_TILE = 512
ATTN_TILE = 256
ATTN_UNROLL = 12
ONES_ROWS = 16
MASK_VALUE = -1e30
LOG2E = math.log2(math.e)


def _rmsnorm(x, g):
    return x * lax.rsqrt(jnp.mean(x * x, axis=-1, keepdims=True) + EPS) * g


def _sigmoid(x):
    return 1.0 / (1.0 + jnp.exp(-x))


def _col_reduce(x, op, groups=8):
    rows, cols = x.shape
    return op(op(x.reshape(groups, rows // groups, cols), axis=0), axis=0, keepdims=True)


def _resident(shape):
    return pl.BlockSpec(shape, lambda *_: (0,) * len(shape), pipeline_mode=pl.Buffered(1))


def _params(*semantics):
    return pltpu.CompilerParams(dimension_semantics=semantics, vmem_limit_bytes=VMEM_LIMIT)


def _conv_layer_kernel(x_ref, ng_ref, w1_ref, b1_ref, dw_ref, dwb_ref, lng_ref, lnb_ref,
                       w2_ref, b2_ref, o_ref, ubuf, cbuf):
    tm, d = x_ref.shape[1], x_ref.shape[2]

    @pl.when(pl.program_id(1) == 0)
    def _():
        ubuf[0:HALO, :] = jnp.zeros((HALO, d), F32)

    x = x_ref[0]
    h = _rmsnorm(x, ng_ref[...]).astype(BF16)
    u = jnp.dot(h, w1_ref[...], preferred_element_type=F32) + b1_ref[...]
    ubuf[HALO:HALO + tm, :] = u[:, :d] * _sigmoid(u[:, d:])

    first = HALO - (CONV_WIDTH - 1)

    def strip(c, carry):
        col = pl.ds(pl.multiple_of(c * LANES, LANES), LANES)
        acc = jnp.broadcast_to(dwb_ref[:, col], (tm, LANES))
        for r in range(SUBLANES):
            rows = tm if r == 0 else tm + SUBLANES
            v = None
            for k in range(first + (r - first) % SUBLANES, first + CONV_WIDTH, SUBLANES):
                term = dw_ref[k - first:k - first + 1, col] * ubuf[k - r:k - r + rows, col]
                v = term if v is None else v + term
            acc = acc + (v if r == 0 else pltpu.roll(v, rows - r, axis=0)[:tm])
        cbuf[:, col] = acc
        return carry

    lax.fori_loop(0, d // LANES, strip, 0)

    ubuf[0:HALO, :] = ubuf[tm:tm + HALO, :]

    c = cbuf[...]
    mu = jnp.mean(c, axis=-1, keepdims=True)
    cc = c - mu
    y = cc * lax.rsqrt(jnp.mean(cc * cc, axis=-1, keepdims=True) + EPS) * lng_ref[...] + lnb_ref[...]
    y = (y * _sigmoid(y)).astype(BF16)
    o_ref[0] = x + jnp.dot(y, w2_ref[...], preferred_element_type=F32) + b2_ref[...]


def _conv_layer(x, ng, w1, b1, dw, dwb, lng, lnb, w2, b2):
    b, s, d = x.shape
    tm = ROW_TILE
    tile = pl.BlockSpec((1, tm, d), lambda bi, si: (bi, si, 0))
    return pl.pallas_call(
        _conv_layer_kernel,
        grid=(b, s // tm),
        in_specs=[tile, _resident((1, d)), _resident((d, 2 * d)), _resident((1, 2 * d)),
                  _resident((CONV_WIDTH, d)), _resident((1, d)), _resident((1, d)),
                  _resident((1, d)), _resident((d, d)), _resident((1, d))],
        out_specs=tile,
        out_shape=jax.ShapeDtypeStruct(x.shape, F32),
        scratch_shapes=[pltpu.VMEM((HALO + tm, d), F32), pltpu.VMEM((tm, d), F32)],
        compiler_params=_params("arbitrary", "arbitrary"),
        name="conv_layer",
    )(x, ng, w1, b1, dw, dwb, lng, lnb, w2, b2)


def _ffn_kernel(*refs, has_attn, final_norm, projs):
    refs = iter(refs)
    x_ref = next(refs)
    if has_attn:
        a_ref, wo_ref = next(refs), next(refs)
    g_ref, w1_ref, w3_ref, w2_ref = next(refs), next(refs), next(refs), next(refs)
    if final_norm:
        fg_ref = next(refs)
    proj_in = [(next(refs), next(refs)) for _ in projs]
    o_ref = next(refs)
    proj_out = [next(refs) for _ in projs]

    x = x_ref[0]
    if has_attn:
        x = x + lax.dot_general(a_ref[0], wo_ref[...], (((0,), (0,)), ((), ())),
                                preferred_element_type=F32)
    h = _rmsnorm(x, g_ref[...]).astype(BF16)
    h1 = jnp.dot(h, w1_ref[...], preferred_element_type=F32)
    h3 = jnp.dot(h, w3_ref[...], preferred_element_type=F32)
    act = ((h1 * _sigmoid(h1)) * h3).astype(BF16)
    y = x + jnp.dot(act, w2_ref[...], preferred_element_type=F32)
    if final_norm:
        y = _rmsnorm(y, fg_ref[...])
    o_ref[0] = y

    if projs:
        yn = y * lax.rsqrt(jnp.mean(y * y, axis=-1, keepdims=True) + EPS)
    for (pg_ref, pw_ref), po_ref, (transposed, scale) in zip(proj_in, proj_out, projs):
        hp = (yn * pg_ref[...]).astype(BF16)
        p = jnp.dot(hp, pw_ref[...], preferred_element_type=F32)
        if scale != 1.0:
            p = p * scale
        po_ref[0] = (p.T if transposed else p).astype(BF16)


def _ffn(x, g, w1, w3, w2, *, attn=None, final_g=None, projs=()):
    b, s, d = x.shape
    f = w1.shape[1]
    tm = FFN_TILE
    tile = pl.BlockSpec((1, tm, d), lambda bi, si: (bi, si, 0))
    args, in_specs = [x], [tile]
    if attn is not None:
        args += list(attn)
        in_specs += [pl.BlockSpec((1, d, tm), lambda bi, si: (bi, 0, si)), _resident(attn[1].shape)]
    args += [g, w1, w3, w2]
    in_specs += [_resident((1, d)), _resident((d, f)), _resident((d, f)), _resident((f, d))]
    if final_g is not None:
        args.append(final_g)
        in_specs.append(_resident((1, d)))
    out_specs, out_shapes = [tile], [jax.ShapeDtypeStruct(x.shape, F32)]
    for pg, pw, transposed, _ in projs:
        args += [pg, pw]
        in_specs += [_resident((1, d)), _resident(pw.shape)]
        if transposed:
            out_specs.append(pl.BlockSpec((1, pw.shape[1], tm), lambda bi, si: (bi, 0, si)))
            out_shapes.append(jax.ShapeDtypeStruct((b, pw.shape[1], s), BF16))
        else:
            out_specs.append(pl.BlockSpec((1, tm, pw.shape[1]), lambda bi, si: (bi, si, 0)))
            out_shapes.append(jax.ShapeDtypeStruct((b, s, pw.shape[1]), BF16))
    return pl.pallas_call(
        functools.partial(_ffn_kernel, has_attn=attn is not None, final_norm=final_g is not None,
                          projs=tuple((t, sc) for _, _, t, sc in projs)),
        grid=(b, s // tm),
        in_specs=in_specs,
        out_specs=out_specs,
        out_shape=out_shapes,
        compiler_params=_params("arbitrary", "arbitrary"),
        name="ffn",
    )(*args)


def _attn_kernel(slopes_ref, itab_ref, jtab_ref, qt_ref, k_ref, vt_ref, lq1_ref, lk1_ref,
                 lq2_ref, lk2_ref, sg_ref, o_ref, bias_scr, qs_scr, s_scr, p_scr, acc_scr,
                 *, lambda_init, n_pairs):
    t_ = ATTN_TILE
    neg_slope2 = -slopes_ref[pl.program_id(0)] * LOG2E

    @pl.when(pl.program_id(1) == 0)
    def _():
        key = lax.broadcasted_iota(jnp.int32, (t_, t_), 0)
        qry = lax.broadcasted_iota(jnp.int32, (t_, t_), 1)
        bias_rel = neg_slope2 * (qry - key).astype(F32)
        bias_scr[0] = bias_rel
        bias_scr[1] = jnp.where(qry >= key, bias_rel, MASK_VALUE)

    @pl.when((pl.program_id(0) == 0) & (pl.program_id(1) == 0))
    def _():
        acc_scr[...] = jnp.zeros(acc_scr.shape, F32)

    feat = lax.broadcasted_iota(jnp.int32, (V_DIM, t_), 0)
    lam = (jnp.exp(jnp.sum(lq1_ref[...] * lk1_ref[...], axis=-1, keepdims=True))
           - jnp.exp(jnp.sum(lq2_ref[...] * lk2_ref[...], axis=-1, keepdims=True))
           + lambda_init)

    def tile(idx):
        return pl.ds(pl.multiple_of(idx * t_, t_), t_)

    for i in range(qt_ref.shape[2] // t_):
        qt = qt_ref[0, :, i * t_:(i + 1) * t_]
        zero = jnp.zeros_like(qt)
        qs_scr[:, 2 * i * t_:(2 * i + 1) * t_] = jnp.where(feat < HEAD_DIM, qt, zero)
        qs_scr[:, (2 * i + 1) * t_:(2 * i + 2) * t_] = jnp.where(feat >= HEAD_DIM, qt, zero)

    def qk(t):
        qs = qs_scr[:, pl.ds(pl.multiple_of(itab_ref[t] * (2 * t_), 2 * t_), 2 * t_)]
        return jnp.dot(k_ref[0, tile(jtab_ref[t]), :], qs, preferred_element_type=F32)

    ones = jnp.ones((ONES_ROWS, t_), BF16)

    def pv(t, slot, alpha):
        i = itab_ref[t]
        vt1 = jnp.concatenate([vt_ref[0, :, tile(jtab_ref[t])], ones], axis=0)
        acc_scr[i] = alpha * acc_scr[i] + jnp.dot(vt1, p_scr[slot], preferred_element_type=F32)

    def softmax(t, slot, m):
        i, j = itab_ref[t], jtab_ref[t]
        b = bias_scr[(i == j).astype(jnp.int32)]
        z = s_scr[slot] + jnp.concatenate([b, b], axis=1)
        c = neg_slope2 * ((i - j) * t_).astype(F32)
        m_old = jnp.where(j == 0, MASK_VALUE, m)
        m_new = jnp.maximum(m_old, _col_reduce(z, jnp.max) + c)
        alpha = jnp.exp2(m_old - m_new)
        p_scr[slot] = jnp.exp2(z - (m_new - c)).astype(BF16)
        return m_new, alpha

    def step(g, carry):
        m, alpha = carry
        for u in range(ATTN_UNROLL):
            t = g * ATTN_UNROLL + u
            s_scr[(u + 1) % 2] = qk(t + 1)
            pv(jnp.maximum(t - 1, 0), (u + 1) % 2, alpha)
            m, alpha = softmax(t, u % 2, m)
        return m, alpha

    row = jnp.zeros((1, 2 * t_), F32)
    p_scr[1] = jnp.zeros(p_scr.shape[1:], BF16)
    s_scr[0] = qk(0)
    assert n_pairs % ATTN_UNROLL == 0 and ATTN_UNROLL % 2 == 0
    _, alpha = lax.fori_loop(0, n_pairs // ATTN_UNROLL, step, (row, row))
    pv(n_pairs - 1, (n_pairs - 1) % 2, alpha)

    gain = jnp.broadcast_to(sg_ref[...], (V_DIM, t_))
    for i in range(acc_scr.shape[0]):
        on = acc_scr[i, :V_DIM, :] * (1.0 / acc_scr[i, V_DIM:V_DIM + 1, :])
        ot = on[:, :t_] - lam * on[:, t_:]
        o = ot * lax.rsqrt(jnp.mean(ot * ot, axis=0, keepdims=True) + EPS) * gain
        o_ref[0, :, i * t_:(i + 1) * t_] = (o * (1.0 - lambda_init)).astype(BF16)


def _attention(qt, k, vt, slopes, lq1, lk1, lq2, lk2, sg, lambda_init):
    b, s, _ = k.shape
    t_ = ATTN_TILE
    nq = s // t_
    pairs = [(i, j) for i in range(nq) for j in range(i + 1)]
    itab = jnp.asarray([i for i, _ in pairs] + [0], jnp.int32)
    jtab = jnp.asarray([j for _, j in pairs] + [0], jnp.int32)
    vec = _resident((1, HEAD_DIM))
    smem = pl.BlockSpec(memory_space=pltpu.SMEM)
    return pl.pallas_call(
        functools.partial(_attn_kernel, lambda_init=lambda_init, n_pairs=len(pairs)),
        grid=(N_HEADS, b),
        in_specs=[smem, smem, smem,
                  pl.BlockSpec((1, V_DIM, s), lambda hi, bi: (bi, hi, 0)),
                  pl.BlockSpec((1, s, V_DIM), lambda hi, bi: (bi, 0, hi)),
                  pl.BlockSpec((1, V_DIM, s), lambda hi, bi: (bi, hi, 0)),
                  vec, vec, vec, vec, _resident((V_DIM, 1))],
        out_specs=pl.BlockSpec((1, V_DIM, s), lambda hi, bi: (bi, hi, 0)),
        out_shape=jax.ShapeDtypeStruct(vt.shape, BF16),
        scratch_shapes=[pltpu.VMEM((2, t_, t_), F32), pltpu.VMEM((V_DIM, 2 * s), BF16),
                        pltpu.VMEM((2, t_, 2 * t_), F32), pltpu.VMEM((2, t_, 2 * t_), BF16),
                        pltpu.VMEM((nq, V_DIM + ONES_ROWS, 2 * t_), F32)],
        compiler_params=_params("arbitrary", "arbitrary"),
        name="diff_attention",
    )(slopes, itab, jtab, qt, k, vt, lq1, lk1, lq2, lk2, sg)


def kernel(x, conv_norm_g, pw1_w, pw1_b, dw_w, dw_b, conv_ln_g, conv_ln_b, pw2_w, pw2_b,
           kv_norm_g, w_k, w_v, attn_norm_g, w_q, lambda_q1, lambda_k1, lambda_q2, lambda_k2,
           subln_g, w_o, ffn_norm_g, ffn_w1, ffn_w3, ffn_w2, final_norm_g):
    b, s, d = x.shape
    depth = ffn_w1.shape[0]
    n_conv = pw1_w.shape[0]
    row = lambda a: a.reshape(1, -1)
    slopes = jnp.asarray((2.0 ** (-8.0 * np.arange(1, N_HEADS + 1) / N_HEADS)).astype(np.float32))
    scale = HEAD_DIM ** -0.5 * LOG2E

    assert 1 <= n_conv < depth
    k = vt = qt = None
    for layer in range(depth):
        attn = None
        if layer < n_conv:
            x = _conv_layer(x, row(conv_norm_g[layer]), pw1_w[layer].astype(BF16), row(pw1_b[layer]),
                            dw_w[layer], row(dw_b[layer]), row(conv_ln_g[layer]),
                            row(conv_ln_b[layer]), pw2_w[layer].astype(BF16), row(pw2_b[layer]))
        else:
            j = layer - n_conv
            lambda_init = 0.8 - 0.6 * math.exp(-0.3 * layer)
            o = _attention(qt, k, vt, slopes, row(lambda_q1[j]), row(lambda_k1[j]),
                           row(lambda_q2[j]), row(lambda_k2[j]), subln_g[j].reshape(-1, 1), lambda_init)
            attn = (o, w_o[j].astype(BF16))
        projs = []
        if layer == n_conv - 1:
            projs += [(row(kv_norm_g), w_k.astype(BF16), False, 1.0),
                      (row(kv_norm_g), w_v.astype(BF16), True, 1.0)]
        if n_conv - 1 <= layer < depth - 1:
            jn = layer + 1 - n_conv
            projs.append((row(attn_norm_g[jn]), w_q[jn].astype(BF16), True, scale))
        outs = _ffn(x, row(ffn_norm_g[layer]), ffn_w1[layer].astype(BF16),
                    ffn_w3[layer].astype(BF16), ffn_w2[layer].astype(BF16), attn=attn,
                    final_g=row(final_norm_g) if layer == depth - 1 else None, projs=projs)
        x = outs[0]
        if layer == n_conv - 1:
            k, vt = outs[1], outs[2]
        if projs:
            qt = outs[-1]
    return x
```

```python
import functools
import math

import jax
import jax.numpy as jnp
import numpy as np
from jax import lax
from jax.experimental import pallas as pl
from jax.experimental.pallas import tpu as pltpu

F32 = jnp.float32
BF16 = jnp.bfloat16

EPS = 1e-6
CONV_WIDTH = 31
N_HEADS = 8
HEAD_DIM = 64
V_DIM = 2 * HEAD_DIM
HALO = 32
LANES = 128
SUBLANES = 8
VMEM_LIMIT = 56 * 1024 * 1024

ROW_TILE = 256
CONV_CHUNK = 256
FFN_TILE = 512
ATTN_TILE = 256
ATTN_UNROLL = 12
ONES_ROWS = 16
MASK_VALUE = -1e30
LOG2E = math.log2(math.e)


def _rmsnorm(x, g):
    return x * lax.rsqrt(jnp.mean(x * x, axis=-1, keepdims=True) + EPS) * g


def _sigmoid(x):
    return 1.0 / (1.0 + jnp.exp(-x))


def _col_reduce(x, op, groups=8):
    rows, cols = x.shape
    return op(op(x.reshape(groups, rows // groups, cols), axis=0), axis=0, keepdims=True)


def _resident(shape):
    return pl.BlockSpec(shape, lambda *_: (0,) * len(shape), pipeline_mode=pl.Buffered(1))


def _layer_spec(stacked, layer):
    return pl.BlockSpec((None,) + stacked.shape[1:], lambda *_: (layer, 0, 0),
                        pipeline_mode=pl.Buffered(1))


def _params(*semantics):
    return pltpu.CompilerParams(dimension_semantics=semantics, vmem_limit_bytes=VMEM_LIMIT)


def _conv_layer_kernel(x_ref, ng_ref, w1_ref, b1_ref, dw_ref, dwb_ref, lng_ref, lnb_ref,
                       w2_ref, b2_ref, o_ref, ubuf, cbuf):
    tm, d = x_ref.shape[1], x_ref.shape[2]

    @pl.when(pl.program_id(1) == 0)
    def _():
        ubuf[0:HALO, :] = jnp.zeros((HALO, d), F32)

    x = x_ref[0]
    h = _rmsnorm(x, ng_ref[...]).astype(BF16)

    def glu(c):
        lo, hi = c * CONV_CHUNK, (c + 1) * CONV_CHUNK
        a = jnp.dot(h, w1_ref[:, lo:hi], preferred_element_type=F32) + b1_ref[:, lo:hi]
        g = jnp.dot(h, w1_ref[:, d + lo:d + hi], preferred_element_type=F32) + b1_ref[:, d + lo:d + hi]
        ubuf[HALO:HALO + tm, lo:hi] = a * _sigmoid(g)

    first = HALO - (CONV_WIDTH - 1)

    def strip(lo):
        col = slice(lo, lo + LANES)
        acc = jnp.broadcast_to(dwb_ref[:, col], (tm, LANES))
        for r in range(SUBLANES):
            rows = tm if r == 0 else tm + SUBLANES
            v = None
            for k in range(first + (r - first) % SUBLANES, first + CONV_WIDTH, SUBLANES):
                term = dw_ref[k - first:k - first + 1, col] * ubuf[k - r:k - r + rows, col]
                v = term if v is None else v + term
            acc = acc + (v if r == 0 else pltpu.roll(v, rows - r, axis=0)[:tm])
        cbuf[:, col] = acc

    n_chunks = d // CONV_CHUNK
    glu(0)
    for c in range(n_chunks):
        if c + 1 < n_chunks:
            glu(c + 1)
        for lo in range(c * CONV_CHUNK, (c + 1) * CONV_CHUNK, LANES):
            strip(lo)

    ubuf[0:HALO, :] = ubuf[tm:tm + HALO, :]

    c = cbuf[...]
    mu = jnp.mean(c, axis=-1, keepdims=True)
    cc = c - mu
    y = cc * lax.rsqrt(jnp.mean(cc * cc, axis=-1, keepdims=True) + EPS) * lng_ref[...] + lnb_ref[...]
    y = (y * _sigmoid(y)).astype(BF16)
    o_ref[0] = x + jnp.dot(y, w2_ref[...], preferred_element_type=F32) + b2_ref[...]


def _conv_layer(x, layer, ng, w1, b1, dw, dwb, lng, lnb, w2, b2):
    b, s, d = x.shape
    tm = ROW_TILE
    tile = pl.BlockSpec((1, tm, d), lambda bi, si: (bi, si, 0))
    return pl.pallas_call(
        _conv_layer_kernel,
        grid=(b, s // tm),
        in_specs=[tile, _resident((1, d)), _layer_spec(w1, layer), _resident((1, 2 * d)),
                  _resident((CONV_WIDTH, d)), _resident((1, d)), _resident((1, d)),
                  _resident((1, d)), _layer_spec(w2, layer), _resident((1, d))],
        out_specs=tile,
        out_shape=jax.ShapeDtypeStruct(x.shape, F32),
        scratch_shapes=[pltpu.VMEM((HALO + tm, d), F32), pltpu.VMEM((tm, d), F32)],
        compiler_params=_params("arbitrary", "arbitrary"),
        name="conv_layer",
    )(x, ng, w1, b1, dw, dwb, lng, lnb, w2, b2)


def _ffn_kernel(*refs, has_attn, final_norm, projs):
    refs = iter(refs)
    x_ref = next(refs)
    if has_attn:
        a_ref, wo_ref = next(refs), next(refs)
    g_ref, w1_ref, w3_ref, w2_ref = next(refs), next(refs), next(refs), next(refs)
    if final_norm:
        fg_ref = next(refs)
    proj_in = [(next(refs), next(refs)) for _ in projs]
    o_ref = next(refs)
    proj_out = [next(refs) for _ in projs]

    x = x_ref[0]
    if has_attn:
        x = x + lax.dot_general(a_ref[0], wo_ref[...], (((0,), (0,)), ((), ())),
                                preferred_element_type=F32)
    h = _rmsnorm(x, g_ref[...]).astype(BF16)
    h1 = jnp.dot(h, w1_ref[...], preferred_element_type=F32)
    h3 = jnp.dot(h, w3_ref[...], preferred_element_type=F32)
    act = ((h1 * _sigmoid(h1)) * h3).astype(BF16)
    y = x + jnp.dot(act, w2_ref[...], preferred_element_type=F32)
    if final_norm:
        y = _rmsnorm(y, fg_ref[...])
    o_ref[0] = y

    if projs:
        yn = y * lax.rsqrt(jnp.mean(y * y, axis=-1, keepdims=True) + EPS)
    for (pg_ref, pw_ref), po_ref, (transposed, scale) in zip(proj_in, proj_out, projs):
        hp = (yn * pg_ref[...]).astype(BF16)
        p = jnp.dot(hp, pw_ref[...], preferred_element_type=F32)
        if scale != 1.0:
            p = p * scale
        po_ref[0] = (p.T if transposed else p).astype(BF16)


def _ffn(x, g, w1, w3, w2, *, attn=None, final_g=None, projs=()):
    b, s, d = x.shape
    tm = FFN_TILE
    tile = pl.BlockSpec((1, tm, d), lambda bi, si: (bi, si, 0))
    args, in_specs = [x], [tile]
    if attn is not None:
        args += [attn[0], attn[1][0]]
        in_specs += [pl.BlockSpec((1, d, tm), lambda bi, si: (bi, 0, si)), _layer_spec(*attn[1])]
    args += [g, w1[0], w3[0], w2[0]]
    in_specs += [_resident((1, d)), _layer_spec(*w1), _layer_spec(*w3), _layer_spec(*w2)]
    if final_g is not None:
        args.append(final_g)
        in_specs.append(_resident((1, d)))
    out_specs, out_shapes = [tile], [jax.ShapeDtypeStruct(x.shape, F32)]
    for pg, pw, transposed, _ in projs:
        n = pw[0].shape[2]
        args += [pg, pw[0]]
        in_specs += [_resident((1, d)), _layer_spec(*pw)]
        if transposed:
            out_specs.append(pl.BlockSpec((1, n, tm), lambda bi, si: (bi, 0, si)))
            out_shapes.append(jax.ShapeDtypeStruct((b, n, s), BF16))
        else:
            out_specs.append(pl.BlockSpec((1, tm, n), lambda bi, si: (bi, si, 0)))
            out_shapes.append(jax.ShapeDtypeStruct((b, s, n), BF16))
    return pl.pallas_call(
        functools.partial(_ffn_kernel, has_attn=attn is not None, final_norm=final_g is not None,
                          projs=tuple((t, sc) for _, _, t, sc in projs)),
        grid=(b, s // tm),
        in_specs=in_specs,
        out_specs=out_specs,
        out_shape=out_shapes,
        compiler_params=_params("arbitrary", "arbitrary"),
        name="ffn",
    )(*args)


def _attn_kernel(slopes_ref, itab_ref, jtab_ref, qt_ref, k_ref, vt_ref, lq1_ref, lk1_ref,
                 lq2_ref, lk2_ref, sg_ref, o_ref, bias_scr, qs_scr, s_scr, p_scr, acc_scr,
                 *, lambda_init, n_pairs):
    t_ = ATTN_TILE
    neg_slope2 = -slopes_ref[pl.program_id(0)] * LOG2E

    @pl.when(pl.program_id(1) == 0)
    def _():
        key = lax.broadcasted_iota(jnp.int32, (t_, t_), 0)
        qry = lax.broadcasted_iota(jnp.int32, (t_, t_), 1)
        bias_rel = neg_slope2 * (qry - key).astype(F32)
        bias_scr[0] = bias_rel
        bias_scr[1] = jnp.where(qry >= key, bias_rel, MASK_VALUE)

    @pl.when((pl.program_id(0) == 0) & (pl.program_id(1) == 0))
    def _():
        acc_scr[...] = jnp.zeros(acc_scr.shape, F32)

    feat = lax.broadcasted_iota(jnp.int32, (V_DIM, t_), 0)
    lam = (jnp.exp(jnp.sum(lq1_ref[...] * lk1_ref[...], axis=-1, keepdims=True))
           - jnp.exp(jnp.sum(lq2_ref[...] * lk2_ref[...], axis=-1, keepdims=True))
           + lambda_init)

    def tile(idx):
        return pl.ds(pl.multiple_of(idx * t_, t_), t_)

    for i in range(qt_ref.shape[2] // t_):
        qt = qt_ref[0, :, i * t_:(i + 1) * t_]
        zero = jnp.zeros_like(qt)
        qs_scr[:, 2 * i * t_:(2 * i + 1) * t_] = jnp.where(feat < HEAD_DIM, qt, zero)
        qs_scr[:, (2 * i + 1) * t_:(2 * i + 2) * t_] = jnp.where(feat >= HEAD_DIM, qt, zero)

    def qk(t):
        qs = qs_scr[:, pl.ds(pl.multiple_of(itab_ref[t] * (2 * t_), 2 * t_), 2 * t_)]
        return jnp.dot(k_ref[0, tile(jtab_ref[t]), :], qs, preferred_element_type=F32)

    ones = jnp.ones((ONES_ROWS, t_), BF16)

    def pv(t, slot, alpha):
        i = itab_ref[t]
        vt1 = jnp.concatenate([vt_ref[0, :, tile(jtab_ref[t])], ones], axis=0)
        acc_scr[i] = alpha * acc_scr[i] + jnp.dot(vt1, p_scr[slot], preferred_element_type=F32)

    def softmax(t, slot, m):
        i, j = itab_ref[t], jtab_ref[t]
        b = bias_scr[(i == j).astype(jnp.int32)]
        z = s_scr[slot] + jnp.concatenate([b, b], axis=1)
        c = neg_slope2 * ((i - j) * t_).astype(F32)
        m_old = jnp.where(j == 0, MASK_VALUE, m)
        m_new = jnp.maximum(m_old, _col_reduce(z, jnp.max) + c)
        alpha = jnp.exp2(m_old - m_new)
        p_scr[slot] = jnp.exp2(z - (m_new - c)).astype(BF16)
        return m_new, alpha

    def step(g, carry):
        m, alpha = carry
        for u in range(ATTN_UNROLL):
            t = g * ATTN_UNROLL + u
            s_scr[(u + 1) % 2] = qk(t + 1)
            pv(jnp.maximum(t - 1, 0), (u + 1) % 2, alpha)
            m, alpha = softmax(t, u % 2, m)
        return m, alpha

    row = jnp.zeros((1, 2 * t_), F32)
    p_scr[1] = jnp.zeros(p_scr.shape[1:], BF16)
    s_scr[0] = qk(0)
    assert n_pairs % ATTN_UNROLL == 0 and ATTN_UNROLL % 2 == 0
    _, alpha = lax.fori_loop(0, n_pairs // ATTN_UNROLL, step, (row, row))
    pv(n_pairs - 1, (n_pairs - 1) % 2, alpha)

    gain = jnp.broadcast_to(sg_ref[...], (V_DIM, t_))
    for i in range(acc_scr.shape[0]):
        on = acc_scr[i, :V_DIM, :] * (1.0 / acc_scr[i, V_DIM:V_DIM + 1, :])
        ot = on[:, :t_] - lam * on[:, t_:]
        o = ot * lax.rsqrt(jnp.mean(ot * ot, axis=0, keepdims=True) + EPS) * gain
        o_ref[0, :, i * t_:(i + 1) * t_] = (o * (1.0 - lambda_init)).astype(BF16)


def _attention(qt, k, vt, slopes, lq1, lk1, lq2, lk2, sg, lambda_init):
    b, s, _ = k.shape
    t_ = ATTN_TILE
    nq = s // t_
    pairs = [(i, j) for i in range(nq) for j in range(i + 1)]
    itab = jnp.asarray([i for i, _ in pairs] + [0], jnp.int32)
    jtab = jnp.asarray([j for _, j in pairs] + [0], jnp.int32)
    vec = _resident((1, HEAD_DIM))
    smem = pl.BlockSpec(memory_space=pltpu.SMEM)
    return pl.pallas_call(
        functools.partial(_attn_kernel, lambda_init=lambda_init, n_pairs=len(pairs)),
        grid=(N_HEADS, b),
        in_specs=[smem, smem, smem,
                  pl.BlockSpec((1, V_DIM, s), lambda hi, bi: (bi, hi, 0)),
                  pl.BlockSpec((1, s, V_DIM), lambda hi, bi: (bi, 0, hi)),
                  pl.BlockSpec((1, V_DIM, s), lambda hi, bi: (bi, hi, 0)),
                  vec, vec, vec, vec, _resident((V_DIM, 1))],
        out_specs=pl.BlockSpec((1, V_DIM, s), lambda hi, bi: (bi, hi, 0)),
        out_shape=jax.ShapeDtypeStruct(vt.shape, BF16),
        scratch_shapes=[pltpu.VMEM((2, t_, t_), F32), pltpu.VMEM((V_DIM, 2 * s), BF16),
                        pltpu.VMEM((2, t_, 2 * t_), F32), pltpu.VMEM((2, t_, 2 * t_), BF16),
                        pltpu.VMEM((nq, V_DIM + ONES_ROWS, 2 * t_), F32)],
        compiler_params=_params("arbitrary", "arbitrary"),
        name="diff_attention",
    )(slopes, itab, jtab, qt, k, vt, lq1, lk1, lq2, lk2, sg)


def kernel(x, conv_norm_g, pw1_w, pw1_b, dw_w, dw_b, conv_ln_g, conv_ln_b, pw2_w, pw2_b,
           kv_norm_g, w_k, w_v, attn_norm_g, w_q, lambda_q1, lambda_k1, lambda_q2, lambda_k2,
           subln_g, w_o, ffn_norm_g, ffn_w1, ffn_w3, ffn_w2, final_norm_g):
    b, s, d = x.shape
    depth = ffn_w1.shape[0]
    n_conv = pw1_w.shape[0]
    row = lambda a: a.reshape(1, -1)
    slopes = jnp.asarray((2.0 ** (-8.0 * np.arange(1, N_HEADS + 1) / N_HEADS)).astype(np.float32))
    scale = HEAD_DIM ** -0.5 * LOG2E

    assert 1 <= n_conv < depth
    pw1_w, pw2_w, w_q, w_o, ffn_w1, ffn_w3, ffn_w2 = (
        a.astype(BF16) for a in (pw1_w, pw2_w, w_q, w_o, ffn_w1, ffn_w3, ffn_w2))
    w_k, w_v = w_k.astype(BF16)[None], w_v.astype(BF16)[None]
    k = vt = qt = None
    for layer in range(depth):
        attn = None
        if layer < n_conv:
            x = _conv_layer(x, layer, row(conv_norm_g[layer]), pw1_w, row(pw1_b[layer]),
                            dw_w[layer], row(dw_b[layer]), row(conv_ln_g[layer]),
                            row(conv_ln_b[layer]), pw2_w, row(pw2_b[layer]))
        else:
            j = layer - n_conv
            lambda_init = 0.8 - 0.6 * math.exp(-0.3 * layer)
            o = _attention(qt, k, vt, slopes, row(lambda_q1[j]), row(lambda_k1[j]),
                           row(lambda_q2[j]), row(lambda_k2[j]), subln_g[j].reshape(-1, 1), lambda_init)
            attn = (o, (w_o, j))
        projs = []
        if layer == n_conv - 1:
            projs += [(row(kv_norm_g), (w_k, 0), False, 1.0), (row(kv_norm_g), (w_v, 0), True, 1.0)]
        if n_conv - 1 <= layer < depth - 1:
            jn = layer + 1 - n_conv
            projs.append((row(attn_norm_g[jn]), (w_q, jn), True, scale))
        outs = _ffn(x, row(ffn_norm_g[layer]), (ffn_w1, layer), (ffn_w3, layer), (ffn_w2, layer),
                    attn=attn, final_g=row(final_norm_g) if layer == depth - 1 else None,
                    projs=projs)
        x = outs[0]
        if layer == n_conv - 1:
            k, vt = outs[1], outs[2]
        if projs:
            qt = outs[-1]
    return x
```

```python
import functools
import math

import jax
import jax.numpy as jnp
import numpy as np
from jax import lax
from jax.experimental import pallas as pl
from jax.experimental.pallas import tpu as pltpu

F32 = jnp.float32
BF16 = jnp.bfloat16

EPS = 1e-6
CONV_WIDTH = 31
N_HEADS = 8
HEAD_DIM = 64
V_DIM = 2 * HEAD_DIM
HALO = 32
LANES = 128
SUBLANES = 8
VMEM_LIMIT = 56 * 1024 * 1024

ROW_TILE = 256
CONV_CHUNK = 256
FFN_TILE = 512
CAST_CHUNKS = 8
ATTN_TILE = 256
ATTN_UNROLL = 12
ATTN_LAG = 1
ONES_ROWS = 16
MASK_VALUE = -1e30
LOG2E = math.log2(math.e)


def _rmsnorm(x, g):
    return x * lax.rsqrt(jnp.mean(x * x, axis=-1, keepdims=True) + EPS) * g


def _sigmoid(x):
    return jax.nn.sigmoid(x)


def _col_reduce(x, op, groups=8):
    rows, cols = x.shape
    return op(op(x.reshape(groups, rows // groups, cols), axis=0), axis=0, keepdims=True)


def _resident(shape):
    return pl.BlockSpec(shape, lambda *_: (0,) * len(shape), pipeline_mode=pl.Buffered(1))


def _layer_spec(stacked, layer):
    return pl.BlockSpec((None,) + stacked.shape[1:], lambda *_: (layer, 0, 0),
                        pipeline_mode=pl.Buffered(1))


def _params(*semantics):
    return pltpu.CompilerParams(dimension_semantics=semantics, vmem_limit_bytes=VMEM_LIMIT)


def _conv_layer_kernel(x_ref, ng_ref, w1_ref, b1_ref, dw_ref, dwb_ref, lng_ref, lnb_ref,
                       w2_ref, b2_ref, o_ref, ubuf, cbuf):
    tm, d = x_ref.shape[1], x_ref.shape[2]

    @pl.when(pl.program_id(1) == 0)
    def _():
        ubuf[0:HALO, :] = jnp.zeros((HALO, d), F32)

    x = x_ref[0]
    h = _rmsnorm(x, ng_ref[...]).astype(BF16)

    def glu(c):
        lo, hi = c * CONV_CHUNK, (c + 1) * CONV_CHUNK
        a = jnp.dot(h, w1_ref[:, lo:hi], preferred_element_type=F32) + b1_ref[:, lo:hi]
        g = jnp.dot(h, w1_ref[:, d + lo:d + hi], preferred_element_type=F32) + b1_ref[:, d + lo:d + hi]
        ubuf[HALO:HALO + tm, lo:hi] = a * _sigmoid(g)

    first = HALO - (CONV_WIDTH - 1)

    def strip(lo):
        col = slice(lo, lo + LANES)
        acc = jnp.broadcast_to(dwb_ref[:, col], (tm, LANES))
        for r in range(SUBLANES):
            rows = tm if r == 0 else tm + SUBLANES
            v = None
            for k in range(first + (r - first) % SUBLANES, first + CONV_WIDTH, SUBLANES):
                term = dw_ref[k - first:k - first + 1, col] * ubuf[k - r:k - r + rows, col]
                v = term if v is None else v + term
            acc = acc + (v if r == 0 else pltpu.roll(v, rows - r, axis=0)[:tm])
        cbuf[:, col] = acc

    n_chunks = d // CONV_CHUNK
    glu(0)
    for c in range(n_chunks):
        if c + 1 < n_chunks:
            glu(c + 1)
        for lo in range(c * CONV_CHUNK, (c + 1) * CONV_CHUNK, LANES):
            strip(lo)

    ubuf[0:HALO, :] = ubuf[tm:tm + HALO, :]

    c = cbuf[...]
    mu = jnp.mean(c, axis=-1, keepdims=True)
    cc = c - mu
    y = cc * lax.rsqrt(jnp.mean(cc * cc, axis=-1, keepdims=True) + EPS) * lng_ref[...] + lnb_ref[...]
    y = (y * _sigmoid(y)).astype(BF16)
    o_ref[0] = x + jnp.dot(y, w2_ref[...], preferred_element_type=F32) + b2_ref[...]


def _conv_layer(x, layer, ng, w1, b1, dw, dwb, lng, lnb, w2, b2):
    b, s, d = x.shape
    tm = ROW_TILE
    tile = pl.BlockSpec((1, tm, d), lambda bi, si: (bi, si, 0))
    return pl.pallas_call(
        _conv_layer_kernel,
        grid=(b, s // tm),
        in_specs=[tile, _resident((1, d)), _layer_spec(w1, layer), _resident((1, 2 * d)),
                  _resident((CONV_WIDTH, d)), _resident((1, d)), _resident((1, d)),
                  _resident((1, d)), _layer_spec(w2, layer), _resident((1, d))],
        out_specs=tile,
        out_shape=jax.ShapeDtypeStruct(x.shape, F32),
        scratch_shapes=[pltpu.VMEM((HALO + tm, d), F32), pltpu.VMEM((tm, d), F32)],
        compiler_params=_params("arbitrary", "arbitrary"),
        name="conv_layer",
    )(x, ng, w1, b1, dw, dwb, lng, lnb, w2, b2)


def _cast_weight(src, dst, stage, sem):
    rows = dst.shape[0]
    step = max(r for r in range(SUBLANES, stage.shape[1] + 1, SUBLANES) if rows % r == 0)

    def copy(k):
        return pltpu.make_async_copy(src.at[pl.ds(k * step, step)],
                                     stage.at[k % 2, pl.ds(0, step)], sem.at[k % 2])

    copy(0).start()
    for k in range(rows // step):
        if (k + 1) * step < rows:
            copy(k + 1).start()
        copy(k).wait()
        dst[k * step:(k + 1) * step, :] = stage[k % 2, :step].astype(BF16)


def _ffn_kernel(*refs, has_attn, final_norm, projs, layers):
    refs = iter(refs)
    x_ref = next(refs)
    if has_attn:
        a_ref, wo_hbm = next(refs), next(refs)
    g_ref, w1_hbm, w3_hbm, w2_hbm = next(refs), next(refs), next(refs), next(refs)
    if final_norm:
        fg_ref = next(refs)
    proj_in = [(next(refs), next(refs)) for _ in projs]
    o_ref = next(refs)
    proj_out = [next(refs) for _ in projs]
    if has_attn:
        wo_ref = next(refs)
    w1_ref, w3_ref, w2_ref = next(refs), next(refs), next(refs)
    proj_w = [next(refs) for _ in projs]
    stage_wide, stage_tall, sem = next(refs), next(refs), next(refs)

    @pl.when((pl.program_id(0) == 0) & (pl.program_id(1) == 0))
    def _():
        hbm = ([wo_hbm] if has_attn else []) + [w1_hbm, w3_hbm, w2_hbm] + [w for _, w in proj_in]
        vmem = ([wo_ref] if has_attn else []) + [w1_ref, w3_ref, w2_ref] + proj_w
        for src, dst, layer in zip(hbm, vmem, layers):
            stage = stage_wide if dst.shape[1] == stage_wide.shape[2] else stage_tall
            _cast_weight(src.at[layer], dst, stage, sem)

    x = x_ref[0]
    if has_attn:
        x = x + lax.dot_general(a_ref[0], wo_ref[...], (((0,), (0,)), ((), ())),
                                preferred_element_type=F32)
    h = _rmsnorm(x, g_ref[...]).astype(BF16)
    h1 = jnp.dot(h, w1_ref[...], preferred_element_type=F32)
    h3 = jnp.dot(h, w3_ref[...], preferred_element_type=F32)
    act = ((h1 * _sigmoid(h1)) * h3).astype(BF16)
    y = x + jnp.dot(act, w2_ref[...], preferred_element_type=F32)
    if final_norm:
        y = _rmsnorm(y, fg_ref[...])
    o_ref[0] = y

    if projs:
        yn = y * lax.rsqrt(jnp.mean(y * y, axis=-1, keepdims=True) + EPS)
    for (pg_ref, _), pw_ref, po_ref, (transposed, scale) in zip(proj_in, proj_w, proj_out, projs):
        hp = (yn * pg_ref[...]).astype(BF16)
        p = jnp.dot(hp, pw_ref[...], preferred_element_type=F32)
        if scale != 1.0:
            p = p * scale
        po_ref[0] = (p.T if transposed else p).astype(BF16)


def _ffn(x, g, w1, w3, w2, *, attn=None, final_g=None, projs=()):
    b, s, d = x.shape
    f = w1[0].shape[2]
    tm = FFN_TILE
    tile = pl.BlockSpec((1, tm, d), lambda bi, si: (bi, si, 0))
    hbm = pl.BlockSpec(memory_space=pl.ANY)
    args, in_specs, weights = [x], [tile], []
    if attn is not None:
        args += [attn[0], attn[1][0]]
        in_specs += [pl.BlockSpec((1, d, tm), lambda bi, si: (bi, 0, si)), hbm]
        weights.append(attn[1])
    args += [g, w1[0], w3[0], w2[0]]
    in_specs += [_resident((1, d)), hbm, hbm, hbm]
    weights += [w1, w3, w2]
    if final_g is not None:
        args.append(final_g)
        in_specs.append(_resident((1, d)))
    out_specs, out_shapes = [tile], [jax.ShapeDtypeStruct(x.shape, F32)]
    for pg, pw, transposed, _ in projs:
        n = pw[0].shape[2]
        args += [pg, pw[0]]
        in_specs += [_resident((1, d)), hbm]
        weights.append(pw)
        if transposed:
            out_specs.append(pl.BlockSpec((1, n, tm), lambda bi, si: (bi, 0, si)))
            out_shapes.append(jax.ShapeDtypeStruct((b, n, s), BF16))
        else:
            out_specs.append(pl.BlockSpec((1, tm, n), lambda bi, si: (bi, si, 0)))
            out_shapes.append(jax.ShapeDtypeStruct((b, s, n), BF16))
    assert all(w.shape[2] in (d, f) for w, _ in weights) and f % CAST_CHUNKS == 0
    scratch = [pltpu.VMEM(w.shape[1:], BF16) for w, _ in weights]
    scratch += [pltpu.VMEM((2, d // CAST_CHUNKS, f), F32), pltpu.VMEM((2, f // CAST_CHUNKS, d), F32),
                pltpu.SemaphoreType.DMA((2,))]
    return pl.pallas_call(
        functools.partial(_ffn_kernel, has_attn=attn is not None, final_norm=final_g is not None,
                          projs=tuple((t, sc) for _, _, t, sc in projs),
                          layers=tuple(layer for _, layer in weights)),
        grid=(b, s // tm),
        in_specs=in_specs,
        out_specs=out_specs,
        out_shape=out_shapes,
        scratch_shapes=scratch,
        compiler_params=_params("arbitrary", "arbitrary"),
        name="ffn",
    )(*args)


def _attn_kernel(slopes_ref, itab_ref, jtab_ref, qt_ref, k_ref, vt_ref, lq1_ref, lk1_ref,
                 lq2_ref, lk2_ref, sg_ref, o_ref, bias_scr, qs_scr, s_scr, p_scr, acc_scr,
                 *, lambda_init, n_pairs):
    t_ = ATTN_TILE
    neg_slope2 = -slopes_ref[pl.program_id(0)] * LOG2E

    @pl.when(pl.program_id(1) == 0)
    def _():
        key = lax.broadcasted_iota(jnp.int32, (t_, t_), 0)
        qry = lax.broadcasted_iota(jnp.int32, (t_, t_), 1)
        bias_rel = neg_slope2 * (qry - key).astype(F32)
        bias_scr[0] = bias_rel
        bias_scr[1] = jnp.where(qry >= key, bias_rel, MASK_VALUE)

    @pl.when((pl.program_id(0) == 0) & (pl.program_id(1) == 0))
    def _():
        acc_scr[...] = jnp.zeros(acc_scr.shape, F32)

    feat = lax.broadcasted_iota(jnp.int32, (V_DIM, t_), 0)
    lam = (jnp.exp(jnp.sum(lq1_ref[...] * lk1_ref[...], axis=-1, keepdims=True))
           - jnp.exp(jnp.sum(lq2_ref[...] * lk2_ref[...], axis=-1, keepdims=True))
           + lambda_init)

    def tile(idx):
        return pl.ds(pl.multiple_of(idx * t_, t_), t_)

    for i in range(qt_ref.shape[2] // t_):
        qt = qt_ref[0, :, i * t_:(i + 1) * t_]
        zero = jnp.zeros_like(qt)
        qs_scr[:, 2 * i * t_:(2 * i + 1) * t_] = jnp.where(feat < HEAD_DIM, qt, zero)
        qs_scr[:, (2 * i + 1) * t_:(2 * i + 2) * t_] = jnp.where(feat >= HEAD_DIM, qt, zero)

    def qk(t):
        qs = qs_scr[:, pl.ds(pl.multiple_of(itab_ref[t] * (2 * t_), 2 * t_), 2 * t_)]
        return jnp.dot(k_ref[0, tile(jtab_ref[t]), :], qs, preferred_element_type=F32)

    ones = jnp.ones((ONES_ROWS, t_), BF16)

    n_slot = ATTN_LAG + 1

    def pv(t, slot, alpha):
        i = itab_ref[t]
        vt1 = jnp.concatenate([vt_ref[0, :, tile(jtab_ref[t])], ones], axis=0)
        acc_scr[i] = alpha * acc_scr[i] + jnp.dot(vt1, p_scr[slot], preferred_element_type=F32)

    def softmax(t, slot, m):
        i, j = itab_ref[t], jtab_ref[t]
        b = bias_scr[(i == j).astype(jnp.int32)]
        z = s_scr[slot] + jnp.concatenate([b, b], axis=1)
        c = neg_slope2 * ((i - j) * t_).astype(F32)
        m_old = jnp.where(j == 0, MASK_VALUE, m)
        m_new = jnp.maximum(m_old, _col_reduce(z, jnp.max) + c)
        alpha = jnp.exp2(m_old - m_new)
        p_scr[slot] = jnp.exp2(z - (m_new - c)).astype(BF16)
        return m_new, alpha

    def step(g, carry):
        m, alphas = carry
        for u in range(ATTN_UNROLL):
            t = g * ATTN_UNROLL + u
            s_scr[(u + ATTN_LAG) % n_slot] = qk(t + ATTN_LAG)
            pv(jnp.maximum(t - ATTN_LAG, 0), (u - ATTN_LAG) % n_slot, alphas[0])
            m, alpha = softmax(t, u % n_slot, m)
            alphas = alphas[1:] + (alpha,)
        return m, alphas

    row = jnp.zeros((1, 2 * t_), F32)
    for lag in range(1, ATTN_LAG + 1):
        p_scr[-lag % n_slot] = jnp.zeros(p_scr.shape[1:], BF16)
        s_scr[ATTN_LAG - lag] = qk(ATTN_LAG - lag)
    assert n_pairs % ATTN_UNROLL == 0 and ATTN_UNROLL % n_slot == 0
    _, alphas = lax.fori_loop(0, n_pairs // ATTN_UNROLL, step, (row, (row,) * ATTN_LAG))
    for lag in range(ATTN_LAG, 0, -1):
        pv(n_pairs - lag, (n_pairs - lag) % n_slot, alphas[ATTN_LAG - lag])

    gain = jnp.broadcast_to(sg_ref[...], (V_DIM, t_))
    for i in range(acc_scr.shape[0]):
        on = acc_scr[i, :V_DIM, :] * (1.0 / acc_scr[i, V_DIM:V_DIM + 1, :])
        ot = on[:, :t_] - lam * on[:, t_:]
        o = ot * lax.rsqrt(jnp.mean(ot * ot, axis=0, keepdims=True) + EPS) * gain
        o_ref[0, :, i * t_:(i + 1) * t_] = (o * (1.0 - lambda_init)).astype(BF16)


def _attention(qt, k, vt, slopes, lq1, lk1, lq2, lk2, sg, lambda_init):
    b, s, _ = k.shape
    t_ = ATTN_TILE
    nq = s // t_
    pairs = [(i, j) for i in range(nq) for j in range(i + 1)]
    itab = jnp.asarray([i for i, _ in pairs] + [0] * ATTN_LAG, jnp.int32)
    jtab = jnp.asarray([j for _, j in pairs] + [0] * ATTN_LAG, jnp.int32)
    vec = _resident((1, HEAD_DIM))
    smem = pl.BlockSpec(memory_space=pltpu.SMEM)
    return pl.pallas_call(
        functools.partial(_attn_kernel, lambda_init=lambda_init, n_pairs=len(pairs)),
        grid=(N_HEADS, b),
        in_specs=[smem, smem, smem,
                  pl.BlockSpec((1, V_DIM, s), lambda hi, bi: (bi, hi, 0)),
                  pl.BlockSpec((1, s, V_DIM), lambda hi, bi: (bi, 0, hi)),
                  pl.BlockSpec((1, V_DIM, s), lambda hi, bi: (bi, hi, 0)),
                  vec, vec, vec, vec, _resident((V_DIM, 1))],
        out_specs=pl.BlockSpec((1, V_DIM, s), lambda hi, bi: (bi, hi, 0)),
        out_shape=jax.ShapeDtypeStruct(vt.shape, BF16),
        scratch_shapes=[pltpu.VMEM((2, t_, t_), F32), pltpu.VMEM((V_DIM, 2 * s), BF16),
                        pltpu.VMEM((ATTN_LAG + 1, t_, 2 * t_), F32),
                        pltpu.VMEM((ATTN_LAG + 1, t_, 2 * t_), BF16),
                        pltpu.VMEM((nq, V_DIM + ONES_ROWS, 2 * t_), F32)],
        compiler_params=_params("arbitrary", "arbitrary"),
        name="diff_attention",
    )(slopes, itab, jtab, qt, k, vt, lq1, lk1, lq2, lk2, sg)


def kernel(x, conv_norm_g, pw1_w, pw1_b, dw_w, dw_b, conv_ln_g, conv_ln_b, pw2_w, pw2_b,
           kv_norm_g, w_k, w_v, attn_norm_g, w_q, lambda_q1, lambda_k1, lambda_q2, lambda_k2,
           subln_g, w_o, ffn_norm_g, ffn_w1, ffn_w3, ffn_w2, final_norm_g):
    b, s, d = x.shape
    depth = ffn_w1.shape[0]
    n_conv = pw1_w.shape[0]
    row = lambda a: a.reshape(1, -1)
    slopes = jnp.asarray((2.0 ** (-8.0 * np.arange(1, N_HEADS + 1) / N_HEADS)).astype(np.float32))
    scale = HEAD_DIM ** -0.5 * LOG2E

    assert 1 <= n_conv < depth
    pw1_w, pw2_w = pw1_w.astype(BF16), pw2_w.astype(BF16)
    w_k, w_v = w_k[None], w_v[None]
    k = vt = qt = None
    for layer in range(depth):
        attn = None
        if layer < n_conv:
            x = _conv_layer(x, layer, row(conv_norm_g[layer]), pw1_w, row(pw1_b[layer]),
                            dw_w[layer], row(dw_b[layer]), row(conv_ln_g[layer]),
                            row(conv_ln_b[layer]), pw2_w, row(pw2_b[layer]))
        else:
            j = layer - n_conv
            lambda_init = 0.8 - 0.6 * math.exp(-0.3 * layer)
            o = _attention(qt, k, vt, slopes, row(lambda_q1[j]), row(lambda_k1[j]),
                           row(lambda_q2[j]), row(lambda_k2[j]), subln_g[j].reshape(-1, 1), lambda_init)
            attn = (o, (w_o, j))
        projs = []
        if layer == n_conv - 1:
            projs += [(row(kv_norm_g), (w_k, 0), False, 1.0), (row(kv_norm_g), (w_v, 0), True, 1.0)]
        if n_conv - 1 <= layer < depth - 1:
            jn = layer + 1 - n_conv
            projs.append((row(attn_norm_g[jn]), (w_q, jn), True, scale))
        outs = _ffn(x, row(ffn_norm_g[layer]), (ffn_w1, layer), (ffn_w3, layer), (ffn_w2, layer),
                    attn=attn, final_g=row(final_norm_g) if layer == depth - 1 else None,
                    projs=projs)
        x = outs[0]
        if layer == n_conv - 1:
            k, vt = outs[1], outs[2]
        if projs:
            qt = outs[-1]
    return x
```

```python
import functools
import math

import jax
import jax.numpy as jnp
import numpy as np
from jax import lax
from jax.experimental import pallas as pl
from jax.experimental.pallas import tpu as pltpu

F32 = jnp.float32
BF16 = jnp.bfloat16

EPS = 1e-6
CONV_WIDTH = 31
N_HEADS = 8
HEAD_DIM = 64
V_DIM = 2 * HEAD_DIM
HALO = 32
LANES = 128
SUBLANES = 8
VMEM_LIMIT = 56 * 1024 * 1024

ROW_TILE = 256
CONV_CHUNK = 256
FFN_TILE = 512
CAST_CHUNKS = 16
CAST_SLOTS = 4
ATTN_TILE = 256
ATTN_UNROLL = 12
ATTN_LAG = 1
ONES_ROWS = 16
MASK_VALUE = -1e30
LOG2E = math.log2(math.e)


def _rmsnorm(x, g):
    return x * lax.rsqrt(jnp.mean(x * x, axis=-1, keepdims=True) + EPS) * g


def _sigmoid(x):
    return jax.nn.sigmoid(x)


def _col_reduce(x, op, groups=8):
    rows, cols = x.shape
    return op(op(x.reshape(groups, rows // groups, cols), axis=0), axis=0, keepdims=True)


def _resident(shape):
    return pl.BlockSpec(shape, lambda *_: (0,) * len(shape), pipeline_mode=pl.Buffered(1))


def _params(*semantics):
    return pltpu.CompilerParams(dimension_semantics=semantics, vmem_limit_bytes=VMEM_LIMIT)


def _conv_layer_kernel(x_ref, ng_ref, w1_hbm, b1_ref, dw_ref, dwb_ref, lng_ref, lnb_ref,
                       w2_hbm, b2_ref, o_ref, ubuf, cbuf, w1_ref, w2_ref, stage1, stage2, sem,
                       *, layer):
    tm, d = x_ref.shape[1], x_ref.shape[2]

    @pl.when((pl.program_id(0) == 0) & (pl.program_id(1) == 0))
    def _():
        _cast_weight(w1_hbm.at[layer], w1_ref, stage1, sem)
        _cast_weight(w2_hbm.at[layer], w2_ref, stage2, sem)

    @pl.when(pl.program_id(1) == 0)
    def _():
        ubuf[0:HALO, :] = jnp.zeros((HALO, d), F32)

    x = x_ref[0]
    h = _rmsnorm(x, ng_ref[...]).astype(BF16)

    def glu(c):
        lo, hi = c * CONV_CHUNK, (c + 1) * CONV_CHUNK
        a = jnp.dot(h, w1_ref[:, lo:hi], preferred_element_type=F32) + b1_ref[:, lo:hi]
        g = jnp.dot(h, w1_ref[:, d + lo:d + hi], preferred_element_type=F32) + b1_ref[:, d + lo:d + hi]
        ubuf[HALO:HALO + tm, lo:hi] = a * _sigmoid(g)

    first = HALO - (CONV_WIDTH - 1)

    def strip(lo):
        col = slice(lo, lo + LANES)
        acc = jnp.broadcast_to(dwb_ref[:, col], (tm, LANES))
        for r in range(SUBLANES):
            rows = tm if r == 0 else tm + SUBLANES
            v = None
            for k in range(first + (r - first) % SUBLANES, first + CONV_WIDTH, SUBLANES):
                term = dw_ref[k - first:k - first + 1, col] * ubuf[k - r:k - r + rows, col]
                v = term if v is None else v + term
            acc = acc + (v if r == 0 else pltpu.roll(v, rows - r, axis=0)[:tm])
        cbuf[:, col] = acc

    n_chunks = d // CONV_CHUNK
    glu(0)
    for c in range(n_chunks):
        if c + 1 < n_chunks:
            glu(c + 1)
        for lo in range(c * CONV_CHUNK, (c + 1) * CONV_CHUNK, LANES):
            strip(lo)

    ubuf[0:HALO, :] = ubuf[tm:tm + HALO, :]

    c = cbuf[...]
    mu = jnp.mean(c, axis=-1, keepdims=True)
    cc = c - mu
    y = cc * lax.rsqrt(jnp.mean(cc * cc, axis=-1, keepdims=True) + EPS) * lng_ref[...] + lnb_ref[...]
    y = (y * _sigmoid(y)).astype(BF16)
    o_ref[0] = x + jnp.dot(y, w2_ref[...], preferred_element_type=F32) + b2_ref[...]


def _conv_layer(x, layer, ng, w1, b1, dw, dwb, lng, lnb, w2, b2):
    b, s, d = x.shape
    tm = ROW_TILE
    tile = pl.BlockSpec((1, tm, d), lambda bi, si: (bi, si, 0))
    hbm = pl.BlockSpec(memory_space=pl.ANY)
    return pl.pallas_call(
        functools.partial(_conv_layer_kernel, layer=layer),
        grid=(b, s // tm),
        in_specs=[tile, _resident((1, d)), hbm, _resident((1, 2 * d)),
                  _resident((CONV_WIDTH, d)), _resident((1, d)), _resident((1, d)),
                  _resident((1, d)), hbm, _resident((1, d))],
        out_specs=tile,
        out_shape=jax.ShapeDtypeStruct(x.shape, F32),
        scratch_shapes=[pltpu.VMEM((HALO + tm, d), F32), pltpu.VMEM((tm, d), F32),
                        pltpu.VMEM(w1.shape[1:], BF16), pltpu.VMEM(w2.shape[1:], BF16),
                        pltpu.VMEM((CAST_SLOTS, d // CAST_CHUNKS, w1.shape[2]), F32),
                        pltpu.VMEM((CAST_SLOTS, d // CAST_CHUNKS, w2.shape[2]), F32),
                        pltpu.SemaphoreType.DMA((CAST_SLOTS,))],
        compiler_params=_params("arbitrary", "arbitrary"),
        name="conv_layer",
    )(x, ng, w1, b1, dw, dwb, lng, lnb, w2, b2)


def _cast_weight(src, dst, stage, sem):
    slots, rows = stage.shape[0], dst.shape[0]
    step = max(r for r in range(SUBLANES, stage.shape[1] + 1, SUBLANES) if rows % r == 0)
    n = rows // step

    def copy(k):
        return pltpu.make_async_copy(src.at[pl.ds(k * step, step)],
                                     stage.at[k % slots, pl.ds(0, step)], sem.at[k % slots])

    for k in range(min(slots - 1, n)):
        copy(k).start()
    for k in range(n):
        if k + slots - 1 < n:
            copy(k + slots - 1).start()
        copy(k).wait()
        dst[k * step:(k + 1) * step, :] = stage[k % slots, :step].astype(BF16)


def _ffn_kernel(*refs, has_attn, final_norm, projs, layers):
    refs = iter(refs)
    x_ref = next(refs)
    if has_attn:
        a_ref, wo_hbm = next(refs), next(refs)
    g_ref, w1_hbm, w3_hbm, w2_hbm = next(refs), next(refs), next(refs), next(refs)
    if final_norm:
        fg_ref = next(refs)
    proj_in = [(next(refs), next(refs)) for _ in projs]
    o_ref = next(refs)
    proj_out = [next(refs) for _ in projs]
    if has_attn:
        wo_ref = next(refs)
    w1_ref, w3_ref, w2_ref = next(refs), next(refs), next(refs)
    proj_w = [next(refs) for _ in projs]
    stage_wide, stage_tall, sem = next(refs), next(refs), next(refs)

    @pl.when((pl.program_id(0) == 0) & (pl.program_id(1) == 0))
    def _():
        hbm = ([wo_hbm] if has_attn else []) + [w1_hbm, w3_hbm, w2_hbm] + [w for _, w in proj_in]
        vmem = ([wo_ref] if has_attn else []) + [w1_ref, w3_ref, w2_ref] + proj_w
        for src, dst, layer in zip(hbm, vmem, layers):
            stage = stage_wide if dst.shape[1] == stage_wide.shape[2] else stage_tall
            _cast_weight(src.at[layer], dst, stage, sem)

    x = x_ref[0]
    if has_attn:
        x = x + lax.dot_general(a_ref[0], wo_ref[...], (((0,), (0,)), ((), ())),
                                preferred_element_type=F32)
    h = _rmsnorm(x, g_ref[...]).astype(BF16)
    h1 = jnp.dot(h, w1_ref[...], preferred_element_type=F32)
    h3 = jnp.dot(h, w3_ref[...], preferred_element_type=F32)
    act = ((h1 * _sigmoid(h1)) * h3).astype(BF16)
    y = x + jnp.dot(act, w2_ref[...], preferred_element_type=F32)
    if final_norm:
        y = _rmsnorm(y, fg_ref[...])
    o_ref[0] = y

    if projs:
        yn = y * lax.rsqrt(jnp.mean(y * y, axis=-1, keepdims=True) + EPS)
    for (pg_ref, _), pw_ref, po_ref, (transposed, scale) in zip(proj_in, proj_w, proj_out, projs):
        hp = (yn * pg_ref[...]).astype(BF16)
        p = jnp.dot(hp, pw_ref[...], preferred_element_type=F32)
        if scale != 1.0:
            p = p * scale
        po_ref[0] = (p.T if transposed else p).astype(BF16)


def _ffn(x, g, w1, w3, w2, *, attn=None, final_g=None, projs=()):
    b, s, d = x.shape
    f = w1[0].shape[2]
    tm = FFN_TILE
    tile = pl.BlockSpec((1, tm, d), lambda bi, si: (bi, si, 0))
    hbm = pl.BlockSpec(memory_space=pl.ANY)
    args, in_specs, weights = [x], [tile], []
    if attn is not None:
        args += [attn[0], attn[1][0]]
        in_specs += [pl.BlockSpec((1, d, tm), lambda bi, si: (bi, 0, si)), hbm]
        weights.append(attn[1])
    args += [g, w1[0], w3[0], w2[0]]
    in_specs += [_resident((1, d)), hbm, hbm, hbm]
    weights += [w1, w3, w2]
    if final_g is not None:
        args.append(final_g)
        in_specs.append(_resident((1, d)))
    out_specs, out_shapes = [tile], [jax.ShapeDtypeStruct(x.shape, F32)]
    for pg, pw, transposed, _ in projs:
        n = pw[0].shape[2]
        args += [pg, pw[0]]
        in_specs += [_resident((1, d)), hbm]
        weights.append(pw)
        if transposed:
            out_specs.append(pl.BlockSpec((1, n, tm), lambda bi, si: (bi, 0, si)))
            out_shapes.append(jax.ShapeDtypeStruct((b, n, s), BF16))
        else:
            out_specs.append(pl.BlockSpec((1, tm, n), lambda bi, si: (bi, si, 0)))
            out_shapes.append(jax.ShapeDtypeStruct((b, s, n), BF16))
    assert all(w.shape[2] in (d, f) for w, _ in weights) and f % CAST_CHUNKS == 0
    scratch = [pltpu.VMEM(w.shape[1:], BF16) for w, _ in weights]
    scratch += [pltpu.VMEM((CAST_SLOTS, d // CAST_CHUNKS, f), F32),
                pltpu.VMEM((CAST_SLOTS, f // CAST_CHUNKS, d), F32),
                pltpu.SemaphoreType.DMA((CAST_SLOTS,))]
    return pl.pallas_call(
        functools.partial(_ffn_kernel, has_attn=attn is not None, final_norm=final_g is not None,
                          projs=tuple((t, sc) for _, _, t, sc in projs),
                          layers=tuple(layer for _, layer in weights)),
        grid=(b, s // tm),
        in_specs=in_specs,
        out_specs=out_specs,
        out_shape=out_shapes,
        scratch_shapes=scratch,
        compiler_params=_params("arbitrary", "arbitrary"),
        name="ffn",
    )(*args)


def _attn_kernel(slopes_ref, itab_ref, jtab_ref, qt_ref, k_ref, vt_ref, lq1_ref, lk1_ref,
                 lq2_ref, lk2_ref, sg_ref, o_ref, bias_scr, qs_scr, s_scr, p_scr, acc_scr,
                 *, lambda_init, n_pairs):
    t_ = ATTN_TILE
    neg_slope2 = -slopes_ref[pl.program_id(0)] * LOG2E

    @pl.when(pl.program_id(1) == 0)
    def _():
        key = lax.broadcasted_iota(jnp.int32, (t_, t_), 0)
        qry = lax.broadcasted_iota(jnp.int32, (t_, t_), 1)
        bias_rel = neg_slope2 * (qry - key).astype(F32)
        bias_scr[0] = bias_rel
        bias_scr[1] = jnp.where(qry >= key, bias_rel, MASK_VALUE)

    @pl.when((pl.program_id(0) == 0) & (pl.program_id(1) == 0))
    def _():
        acc_scr[...] = jnp.zeros(acc_scr.shape, F32)

    feat = lax.broadcasted_iota(jnp.int32, (V_DIM, t_), 0)
    lam = (jnp.exp(jnp.sum(lq1_ref[...] * lk1_ref[...], axis=-1, keepdims=True))
           - jnp.exp(jnp.sum(lq2_ref[...] * lk2_ref[...], axis=-1, keepdims=True))
           + lambda_init)

    def tile(idx):
        return pl.ds(pl.multiple_of(idx * t_, t_), t_)

    for i in range(qt_ref.shape[2] // t_):
        qt = qt_ref[0, :, i * t_:(i + 1) * t_]
        zero = jnp.zeros_like(qt)
        qs_scr[:, 2 * i * t_:(2 * i + 1) * t_] = jnp.where(feat < HEAD_DIM, qt, zero)
        qs_scr[:, (2 * i + 1) * t_:(2 * i + 2) * t_] = jnp.where(feat >= HEAD_DIM, qt, zero)

    def qk(t):
        qs = qs_scr[:, pl.ds(pl.multiple_of(itab_ref[t] * (2 * t_), 2 * t_), 2 * t_)]
        return jnp.dot(k_ref[0, tile(jtab_ref[t]), :], qs, preferred_element_type=F32)

    ones = jnp.ones((ONES_ROWS, t_), BF16)

    n_slot = ATTN_LAG + 1

    def pv(t, slot, alpha):
        i = itab_ref[t]
        vt1 = jnp.concatenate([vt_ref[0, :, tile(jtab_ref[t])], ones], axis=0)
        acc_scr[i] = alpha * acc_scr[i] + jnp.dot(vt1, p_scr[slot], preferred_element_type=F32)

    def softmax(t, slot, m):
        i, j = itab_ref[t], jtab_ref[t]
        b = bias_scr[(i == j).astype(jnp.int32)]
        z = s_scr[slot] + jnp.concatenate([b, b], axis=1)
        c = neg_slope2 * ((i - j) * t_).astype(F32)
        m_old = jnp.where(j == 0, MASK_VALUE, m)
        m_new = jnp.maximum(m_old, _col_reduce(z, jnp.max) + c)
        alpha = jnp.exp2(m_old - m_new)
        p_scr[slot] = jnp.exp2(z - (m_new - c)).astype(BF16)
        return m_new, alpha

    def step(g, carry):
        m, alphas = carry
        for u in range(ATTN_UNROLL):
            t = g * ATTN_UNROLL + u
            s_scr[(u + ATTN_LAG) % n_slot] = qk(t + ATTN_LAG)
            pv(jnp.maximum(t - ATTN_LAG, 0), (u - ATTN_LAG) % n_slot, alphas[0])
            m, alpha = softmax(t, u % n_slot, m)
            alphas = alphas[1:] + (alpha,)
        return m, alphas

    row = jnp.zeros((1, 2 * t_), F32)
    for lag in range(1, ATTN_LAG + 1):
        p_scr[-lag % n_slot] = jnp.zeros(p_scr.shape[1:], BF16)
        s_scr[ATTN_LAG - lag] = qk(ATTN_LAG - lag)
    assert n_pairs % ATTN_UNROLL == 0 and ATTN_UNROLL % n_slot == 0
    _, alphas = lax.fori_loop(0, n_pairs // ATTN_UNROLL, step, (row, (row,) * ATTN_LAG))
    for lag in range(ATTN_LAG, 0, -1):
        pv(n_pairs - lag, (n_pairs - lag) % n_slot, alphas[ATTN_LAG - lag])

    gain = jnp.broadcast_to(sg_ref[...], (V_DIM, t_))
    for i in range(acc_scr.shape[0]):
        on = acc_scr[i, :V_DIM, :] * (1.0 / acc_scr[i, V_DIM:V_DIM + 1, :])
        ot = on[:, :t_] - lam * on[:, t_:]
        o = ot * lax.rsqrt(jnp.mean(ot * ot, axis=0, keepdims=True) + EPS) * gain
        o_ref[0, :, i * t_:(i + 1) * t_] = (o * (1.0 - lambda_init)).astype(BF16)


def _attention(qt, k, vt, slopes, lq1, lk1, lq2, lk2, sg, lambda_init):
    b, s, _ = k.shape
    t_ = ATTN_TILE
    nq = s // t_
    pairs = [(i, j) for i in range(nq) for j in range(i + 1)]
    itab = jnp.asarray([i for i, _ in pairs] + [0] * ATTN_LAG, jnp.int32)
    jtab = jnp.asarray([j for _, j in pairs] + [0] * ATTN_LAG, jnp.int32)
    vec = _resident((1, HEAD_DIM))
    smem = pl.BlockSpec(memory_space=pltpu.SMEM)
    return pl.pallas_call(
        functools.partial(_attn_kernel, lambda_init=lambda_init, n_pairs=len(pairs)),
        grid=(N_HEADS, b),
        in_specs=[smem, smem, smem,
                  pl.BlockSpec((1, V_DIM, s), lambda hi, bi: (bi, hi, 0)),
                  pl.BlockSpec((1, s, V_DIM), lambda hi, bi: (bi, 0, hi)),
                  pl.BlockSpec((1, V_DIM, s), lambda hi, bi: (bi, hi, 0)),
                  vec, vec, vec, vec, _resident((V_DIM, 1))],
        out_specs=pl.BlockSpec((1, V_DIM, s), lambda hi, bi: (bi, hi, 0)),
        out_shape=jax.ShapeDtypeStruct(vt.shape, BF16),
        scratch_shapes=[pltpu.VMEM((2, t_, t_), F32), pltpu.VMEM((V_DIM, 2 * s), BF16),
                        pltpu.VMEM((ATTN_LAG + 1, t_, 2 * t_), F32),
                        pltpu.VMEM((ATTN_LAG + 1, t_, 2 * t_), BF16),
                        pltpu.VMEM((nq, V_DIM + ONES_ROWS, 2 * t_), F32)],
        compiler_params=_params("arbitrary", "arbitrary"),
        name="diff_attention",
    )(slopes, itab, jtab, qt, k, vt, lq1, lk1, lq2, lk2, sg)


def kernel(x, conv_norm_g, pw1_w, pw1_b, dw_w, dw_b, conv_ln_g, conv_ln_b, pw2_w, pw2_b,
           kv_norm_g, w_k, w_v, attn_norm_g, w_q, lambda_q1, lambda_k1, lambda_q2, lambda_k2,
           subln_g, w_o, ffn_norm_g, ffn_w1, ffn_w3, ffn_w2, final_norm_g):
    b, s, d = x.shape
    depth = ffn_w1.shape[0]
    n_conv = pw1_w.shape[0]
    row = lambda a: a.reshape(1, -1)
    slopes = jnp.asarray((2.0 ** (-8.0 * np.arange(1, N_HEADS + 1) / N_HEADS)).astype(np.float32))
    scale = HEAD_DIM ** -0.5 * LOG2E

    assert 1 <= n_conv < depth
    w_k, w_v = w_k[None], w_v[None]
    k = vt = qt = None
    for layer in range(depth):
        attn = None
        if layer < n_conv:
            x = _conv_layer(x, layer, row(conv_norm_g[layer]), pw1_w, row(pw1_b[layer]),
                            dw_w[layer], row(dw_b[layer]), row(conv_ln_g[layer]),
                            row(conv_ln_b[layer]), pw2_w, row(pw2_b[layer]))
        else:
            j = layer - n_conv
            lambda_init = 0.8 - 0.6 * math.exp(-0.3 * layer)
            o = _attention(qt, k, vt, slopes, row(lambda_q1[j]), row(lambda_k1[j]),
                           row(lambda_q2[j]), row(lambda_k2[j]), subln_g[j].reshape(-1, 1), lambda_init)
            attn = (o, (w_o, j))
        projs = []
        if layer == n_conv - 1:
            projs += [(row(kv_norm_g), (w_k, 0), False, 1.0), (row(kv_norm_g), (w_v, 0), True, 1.0)]
        if n_conv - 1 <= layer < depth - 1:
            jn = layer + 1 - n_conv
            projs.append((row(attn_norm_g[jn]), (w_q, jn), True, scale))
        outs = _ffn(x, row(ffn_norm_g[layer]), (ffn_w1, layer), (ffn_w3, layer), (ffn_w2, layer),
                    attn=attn, final_g=row(final_norm_g) if layer == depth - 1 else None,
                    projs=projs)
        x = outs[0]
        if layer == n_conv - 1:
            k, vt = outs[1], outs[2]
        if projs:
            qt = outs[-1]
    return x
```

```python
import functools
import math

import jax
import jax.numpy as jnp
import numpy as np
from jax import lax
from jax.experimental import pallas as pl
from jax.experimental.pallas import tpu as pltpu

F32 = jnp.float32
BF16 = jnp.bfloat16

EPS = 1e-6
CONV_WIDTH = 31
N_HEADS = 8
HEAD_DIM = 64
V_DIM = 2 * HEAD_DIM
HALO = 32
LANES = 128
SUBLANES = 8
VMEM_LIMIT = 56 * 1024 * 1024

ROW_TILE = 256
CONV_CHUNK = 256
FFN_TILE = 512
CAST_CHUNKS = 16
CAST_SLOTS = 4
ATTN_TILE = 256
ONES_ROWS = 16
MASK_VALUE = -1e30
LOG2E = math.log2(math.e)


def _rmsnorm(x, g):
    return x * lax.rsqrt(jnp.mean(x * x, axis=-1, keepdims=True) + EPS) * g


def _sigmoid(x):
    return jax.nn.sigmoid(x)


def _col_reduce(x, op, groups=8):
    rows, cols = x.shape
    return op(op(x.reshape(groups, rows // groups, cols), axis=0), axis=0, keepdims=True)


def _resident(shape):
    return pl.BlockSpec(shape, lambda *_: (0,) * len(shape), pipeline_mode=pl.Buffered(1))


def _params(*semantics):
    return pltpu.CompilerParams(dimension_semantics=semantics, vmem_limit_bytes=VMEM_LIMIT)


def _cast_weight(src, dst, stage, sem):
    slots, rows = stage.shape[0], dst.shape[0]
    step = max(r for r in range(SUBLANES, stage.shape[1] + 1, SUBLANES) if rows % r == 0)
    n = rows // step

    def copy(k):
        return pltpu.make_async_copy(src.at[pl.ds(k * step, step)],
                                     stage.at[k % slots, pl.ds(0, step)], sem.at[k % slots])

    for k in range(min(slots - 1, n)):
        copy(k).start()
    for k in range(n):
        if k + slots - 1 < n:
            copy(k + slots - 1).start()
        copy(k).wait()
        dst[k * step:(k + 1) * step, :] = stage[k % slots, :step].astype(BF16)


def _conv_layer_kernel(x_ref, ng_ref, w1_hbm, b1_ref, dw_ref, dwb_ref, lng_ref, lnb_ref,
                       w2_hbm, b2_ref, o_ref, ubuf, cbuf, w1_ref, w2_ref, stage1, stage2, sem,
                       *, layer):
    tm, d = x_ref.shape[1], x_ref.shape[2]

    @pl.when((pl.program_id(0) == 0) & (pl.program_id(1) == 0))
    def _():
        _cast_weight(w1_hbm.at[layer], w1_ref, stage1, sem)
        _cast_weight(w2_hbm.at[layer], w2_ref, stage2, sem)

    @pl.when(pl.program_id(1) == 0)
    def _():
        ubuf[0:HALO, :] = jnp.zeros((HALO, d), F32)

    x = x_ref[0]
    h = _rmsnorm(x, ng_ref[...]).astype(BF16)

    def glu(c):
        lo, hi = c * CONV_CHUNK, (c + 1) * CONV_CHUNK
        a = jnp.dot(h, w1_ref[:, lo:hi], preferred_element_type=F32) + b1_ref[:, lo:hi]
        g = jnp.dot(h, w1_ref[:, d + lo:d + hi], preferred_element_type=F32) + b1_ref[:, d + lo:d + hi]
        ubuf[HALO:HALO + tm, lo:hi] = a * _sigmoid(g)

    first = HALO - (CONV_WIDTH - 1)

    def strip(lo):
        col = slice(lo, lo + LANES)
        acc = jnp.broadcast_to(dwb_ref[:, col], (tm, LANES))
        for r in range(SUBLANES):
            rows = tm if r == 0 else tm + SUBLANES
            v = None
            for k in range(first + (r - first) % SUBLANES, first + CONV_WIDTH, SUBLANES):
                term = dw_ref[k - first:k - first + 1, col] * ubuf[k - r:k - r + rows, col]
                v = term if v is None else v + term
            acc = acc + (v if r == 0 else pltpu.roll(v, rows - r, axis=0)[:tm])
        cbuf[:, col] = acc

    n_chunks = d // CONV_CHUNK
    glu(0)
    for c in range(n_chunks):
        if c + 1 < n_chunks:
            glu(c + 1)
        for lo in range(c * CONV_CHUNK, (c + 1) * CONV_CHUNK, LANES):
            strip(lo)

    ubuf[0:HALO, :] = ubuf[tm:tm + HALO, :]

    c = cbuf[...]
    mu = jnp.mean(c, axis=-1, keepdims=True)
    cc = c - mu
    y = cc * lax.rsqrt(jnp.mean(cc * cc, axis=-1, keepdims=True) + EPS) * lng_ref[...] + lnb_ref[...]
    y = (y * _sigmoid(y)).astype(BF16)
    o_ref[0] = x + jnp.dot(y, w2_ref[...], preferred_element_type=F32) + b2_ref[...]


def _conv_layer(x, layer, ng, w1, b1, dw, dwb, lng, lnb, w2, b2):
    b, s, d = x.shape
    tm = ROW_TILE
    tile = pl.BlockSpec((1, tm, d), lambda bi, si: (bi, si, 0))
    hbm = pl.BlockSpec(memory_space=pl.ANY)
    return pl.pallas_call(
        functools.partial(_conv_layer_kernel, layer=layer),
        grid=(b, s // tm),
        in_specs=[tile, _resident((1, d)), hbm, _resident((1, 2 * d)),
                  _resident((CONV_WIDTH, d)), _resident((1, d)), _resident((1, d)),
                  _resident((1, d)), hbm, _resident((1, d))],
        out_specs=tile,
        out_shape=jax.ShapeDtypeStruct(x.shape, F32),
        scratch_shapes=[pltpu.VMEM((HALO + tm, d), F32), pltpu.VMEM((tm, d), F32),
                        pltpu.VMEM(w1.shape[1:], BF16), pltpu.VMEM(w2.shape[1:], BF16),
                        pltpu.VMEM((CAST_SLOTS, d // CAST_CHUNKS, w1.shape[2]), F32),
                        pltpu.VMEM((CAST_SLOTS, d // CAST_CHUNKS, w2.shape[2]), F32),
                        pltpu.SemaphoreType.DMA((CAST_SLOTS,))],
        compiler_params=_params("arbitrary", "arbitrary"),
        name="conv_layer",
    )(x, ng, w1, b1, dw, dwb, lng, lnb, w2, b2)


def _ffn_kernel(*refs, has_attn, final_norm, projs, layers):
    refs = iter(refs)
    x_ref = next(refs)
    if has_attn:
        a_ref, wo_hbm = next(refs), next(refs)
    g_ref, w1_hbm, w3_hbm, w2_hbm = next(refs), next(refs), next(refs), next(refs)
    if final_norm:
        fg_ref = next(refs)
    proj_in = [(next(refs), next(refs)) for _ in projs]
    o_ref = next(refs)
    proj_out = [next(refs) for _ in projs]
    if has_attn:
        wo_ref = next(refs)
    w1_ref, w3_ref, w2_ref = next(refs), next(refs), next(refs)
    proj_w = [next(refs) for _ in projs]
    stage_wide, stage_tall, sem = next(refs), next(refs), next(refs)

    @pl.when((pl.program_id(0) == 0) & (pl.program_id(1) == 0))
    def _():
        hbm = ([wo_hbm] if has_attn else []) + [w1_hbm, w3_hbm, w2_hbm] + [w for _, w in proj_in]
        vmem = ([wo_ref] if has_attn else []) + [w1_ref, w3_ref, w2_ref] + proj_w
        for src, dst, layer in zip(hbm, vmem, layers):
            stage = stage_wide if dst.shape[1] == stage_wide.shape[2] else stage_tall
            _cast_weight(src.at[layer], dst, stage, sem)

    x = x_ref[0]
    if has_attn:
        x = x + lax.dot_general(a_ref[0], wo_ref[...], (((0,), (0,)), ((), ())),
                                preferred_element_type=F32)
    h = _rmsnorm(x, g_ref[...]).astype(BF16)
    h1 = jnp.dot(h, w1_ref[...], preferred_element_type=F32)
    h3 = jnp.dot(h, w3_ref[...], preferred_element_type=F32)
    act = ((h1 * _sigmoid(h1)) * h3).astype(BF16)
    y = x + jnp.dot(act, w2_ref[...], preferred_element_type=F32)
    if final_norm:
        y = _rmsnorm(y, fg_ref[...])
    o_ref[0] = y

    if projs:
        yn = y * lax.rsqrt(jnp.mean(y * y, axis=-1, keepdims=True) + EPS)
    for (pg_ref, _), pw_ref, po_ref, (transposed, scale) in zip(proj_in, proj_w, proj_out, projs):
        hp = (yn * pg_ref[...]).astype(BF16)
        p = jnp.dot(hp, pw_ref[...], preferred_element_type=F32)
        if scale != 1.0:
            p = p * scale
        po_ref[0] = (p.T if transposed else p).astype(BF16)


def _ffn(x, g, w1, w3, w2, *, attn=None, final_g=None, projs=()):
    b, s, d = x.shape
    f = w1[0].shape[2]
    tm = FFN_TILE
    tile = pl.BlockSpec((1, tm, d), lambda bi, si: (bi, si, 0))
    hbm = pl.BlockSpec(memory_space=pl.ANY)
    args, in_specs, weights = [x], [tile], []
    if attn is not None:
        args += [attn[0], attn[1][0]]
        in_specs += [pl.BlockSpec((1, d, tm), lambda bi, si: (bi, 0, si)), hbm]
        weights.append(attn[1])
    args += [g, w1[0], w3[0], w2[0]]
    in_specs += [_resident((1, d)), hbm, hbm, hbm]
    weights += [w1, w3, w2]
    if final_g is not None:
        args.append(final_g)
        in_specs.append(_resident((1, d)))
    out_specs, out_shapes = [tile], [jax.ShapeDtypeStruct(x.shape, F32)]
    for pg, pw, transposed, _ in projs:
        n = pw[0].shape[2]
        args += [pg, pw[0]]
        in_specs += [_resident((1, d)), hbm]
        weights.append(pw)
        if transposed:
            out_specs.append(pl.BlockSpec((1, n, tm), lambda bi, si: (bi, 0, si)))
            out_shapes.append(jax.ShapeDtypeStruct((b, n, s), BF16))
        else:
            out_specs.append(pl.BlockSpec((1, tm, n), lambda bi, si: (bi, si, 0)))
            out_shapes.append(jax.ShapeDtypeStruct((b, s, n), BF16))
    assert all(w.shape[2] in (d, f) for w, _ in weights) and f % CAST_CHUNKS == 0
    scratch = [pltpu.VMEM(w.shape[1:], BF16) for w, _ in weights]
    scratch += [pltpu.VMEM((CAST_SLOTS, d // CAST_CHUNKS, f), F32),
                pltpu.VMEM((CAST_SLOTS, f // CAST_CHUNKS, d), F32),
                pltpu.SemaphoreType.DMA((CAST_SLOTS,))]
    return pl.pallas_call(
        functools.partial(_ffn_kernel, has_attn=attn is not None, final_norm=final_g is not None,
                          projs=tuple((t, sc) for _, _, t, sc in projs),
                          layers=tuple(layer for _, layer in weights)),
        grid=(b, s // tm),
        in_specs=in_specs,
        out_specs=out_specs,
        out_shape=out_shapes,
        scratch_shapes=scratch,
        compiler_params=_params("arbitrary", "arbitrary"),
        name="ffn",
    )(*args)


def _attn_kernel(slopes_ref, qt_ref, k_ref, vt_ref, lq1_ref, lk1_ref, lq2_ref, lk2_ref, sg_ref,
                 o_ref, bias_scr, qs_scr, s_scr, p_scr, acc_scr, *, lambda_init):
    t_ = ATTN_TILE
    nq = qt_ref.shape[2] // t_
    neg_slope2 = -slopes_ref[pl.program_id(0)] * LOG2E

    @pl.when(pl.program_id(1) == 0)
    def _():
        key = lax.broadcasted_iota(jnp.int32, (t_, t_), 0)
        qry = lax.broadcasted_iota(jnp.int32, (t_, t_), 1)
        bias_rel = neg_slope2 * (qry - key).astype(F32)
        bias_scr[0] = bias_rel
        bias_scr[1] = jnp.where(qry >= key, bias_rel, MASK_VALUE)

    feat = lax.broadcasted_iota(jnp.int32, (V_DIM, t_), 0)
    lam = (jnp.exp(jnp.sum(lq1_ref[...] * lk1_ref[...], axis=-1, keepdims=True))
           - jnp.exp(jnp.sum(lq2_ref[...] * lk2_ref[...], axis=-1, keepdims=True))
           + lambda_init)
    gain = jnp.broadcast_to(sg_ref[...], (V_DIM, t_))

    def tile(idx):
        return slice(idx * t_, (idx + 1) * t_)

    for i in range(nq):
        qt = qt_ref[0, :, tile(i)]
        zero = jnp.zeros_like(qt)
        qs_scr[:, tile(2 * i)] = jnp.where(feat < HEAD_DIM, qt, zero)
        qs_scr[:, tile(2 * i + 1)] = jnp.where(feat >= HEAD_DIM, qt, zero)

    def qk(i, j):
        return jnp.dot(k_ref[0, tile(j), :], qs_scr[:, 2 * i * t_:(2 * i + 2) * t_],
                       preferred_element_type=F32)

    ones = jnp.ones((ONES_ROWS, t_), BF16)

    def pv(i, j, slot, alpha):
        vt1 = jnp.concatenate([vt_ref[0, :, tile(j)], ones], axis=0)
        new = jnp.dot(vt1, p_scr[slot], preferred_element_type=F32)
        acc_scr[i] = new if j == 0 else alpha * acc_scr[i] + new

    def softmax(i, j, slot, m):
        b = bias_scr[1 if i == j else 0]
        z = s_scr[slot] + jnp.concatenate([b, b], axis=1)
        c = neg_slope2 * float((i - j) * t_)
        m_loc = _col_reduce(z, jnp.max) + c
        m_new = m_loc if j == 0 else jnp.maximum(m, m_loc)
        alpha = None if j == 0 else jnp.exp2(m - m_new)
        p_scr[slot] = jnp.exp2(z - (m_new - c)).astype(BF16)
        return m_new, alpha

    def finish(i):
        on = acc_scr[i, :V_DIM, :] * (1.0 / acc_scr[i, V_DIM:V_DIM + 1, :])
        ot = on[:, :t_] - lam * on[:, t_:]
        o = ot * lax.rsqrt(jnp.mean(ot * ot, axis=0, keepdims=True) + EPS) * gain
        o_ref[0, :, tile(i)] = (o * (1.0 - lambda_init)).astype(BF16)

    pairs = [(i, j) for i in range(nq) for j in range(i + 1)]
    s_scr[0] = qk(*pairs[0])
    m = alpha = None
    for t in range(len(pairs) + 1):
        if t + 1 < len(pairs):
            s_scr[(t + 1) % 2] = qk(*pairs[t + 1])
        if t > 0:
            ip, jp = pairs[t - 1]
            pv(ip, jp, (t - 1) % 2, alpha)
            if ip == jp:
                finish(ip)
        if t < len(pairs):
            m, alpha = softmax(*pairs[t], t % 2, m)


def _attention(qt, k, vt, slopes, lq1, lk1, lq2, lk2, sg, lambda_init):
    b, s, _ = k.shape
    t_ = ATTN_TILE
    nq = s // t_
    vec = _resident((1, HEAD_DIM))
    return pl.pallas_call(
        functools.partial(_attn_kernel, lambda_init=lambda_init),
        grid=(N_HEADS, b),
        in_specs=[pl.BlockSpec(memory_space=pltpu.SMEM),
                  pl.BlockSpec((1, V_DIM, s), lambda hi, bi: (bi, hi, 0)),
                  pl.BlockSpec((1, s, V_DIM), lambda hi, bi: (bi, 0, hi)),
                  pl.BlockSpec((1, V_DIM, s), lambda hi, bi: (bi, hi, 0)),
                  vec, vec, vec, vec, _resident((V_DIM, 1))],
        out_specs=pl.BlockSpec((1, V_DIM, s), lambda hi, bi: (bi, hi, 0)),
        out_shape=jax.ShapeDtypeStruct(vt.shape, BF16),
        scratch_shapes=[pltpu.VMEM((2, t_, t_), F32), pltpu.VMEM((V_DIM, 2 * s), BF16),
                        pltpu.VMEM((2, t_, 2 * t_), F32), pltpu.VMEM((2, t_, 2 * t_), BF16),
                        pltpu.VMEM((nq, V_DIM + ONES_ROWS, 2 * t_), F32)],
        compiler_params=_params("arbitrary", "arbitrary"),
        name="diff_attention",
    )(slopes, qt, k, vt, lq1, lk1, lq2, lk2, sg)


def kernel(x, conv_norm_g, pw1_w, pw1_b, dw_w, dw_b, conv_ln_g, conv_ln_b, pw2_w, pw2_b,
           kv_norm_g, w_k, w_v, attn_norm_g, w_q, lambda_q1, lambda_k1, lambda_q2, lambda_k2,
           subln_g, w_o, ffn_norm_g, ffn_w1, ffn_w3, ffn_w2, final_norm_g):
    b, s, d = x.shape
    depth = ffn_w1.shape[0]
    n_conv = pw1_w.shape[0]
    row = lambda a: a.reshape(1, -1)
    slopes = jnp.asarray((2.0 ** (-8.0 * np.arange(1, N_HEADS + 1) / N_HEADS)).astype(np.float32))
    scale = HEAD_DIM ** -0.5 * LOG2E

    assert 1 <= n_conv < depth
    w_k, w_v = w_k[None], w_v[None]
    k = vt = qt = None
    for layer in range(depth):
        attn = None
        if layer < n_conv:
            x = _conv_layer(x, layer, row(conv_norm_g[layer]), pw1_w, row(pw1_b[layer]),
                            dw_w[layer], row(dw_b[layer]), row(conv_ln_g[layer]),
                            row(conv_ln_b[layer]), pw2_w, row(pw2_b[layer]))
        else:
            j = layer - n_conv
            lambda_init = 0.8 - 0.6 * math.exp(-0.3 * layer)
            o = _attention(qt, k, vt, slopes, row(lambda_q1[j]), row(lambda_k1[j]),
                           row(lambda_q2[j]), row(lambda_k2[j]), subln_g[j].reshape(-1, 1), lambda_init)
            attn = (o, (w_o, j))
        projs = []
        if layer == n_conv - 1:
            projs += [(row(kv_norm_g), (w_k, 0), False, 1.0), (row(kv_norm_g), (w_v, 0), True, 1.0)]
        if n_conv - 1 <= layer < depth - 1:
            jn = layer + 1 - n_conv
            projs.append((row(attn_norm_g[jn]), (w_q, jn), True, scale))
        outs = _ffn(x, row(ffn_norm_g[layer]), (ffn_w1, layer), (ffn_w3, layer), (ffn_w2, layer),
                    attn=attn, final_g=row(final_norm_g) if layer == depth - 1 else None,
                    projs=projs)
        x = outs[0]
        if layer == n_conv - 1:
            k, vt = outs[1], outs[2]
        if projs:
            qt = outs[-1]
    return x
```

```python
import functools
import math

import jax
import jax.numpy as jnp
import numpy as np
from jax import lax
from jax.experimental import pallas as pl
from jax.experimental.pallas import tpu as pltpu

F32 = jnp.float32
BF16 = jnp.bfloat16

EPS = 1e-6
CONV_WIDTH = 31
N_HEADS = 8
HEAD_DIM = 64
V_DIM = 2 * HEAD_DIM
HALO = 32
LANES = 128
SUBLANES = 8
VMEM_LIMIT = 56 * 1024 * 1024

ROW_TILE = 256
CONV_CHUNK = 256
FFN_TILE = 512
CAST_CHUNKS = 16
CAST_SLOTS = 4
ATTN_TILE = 256
ONES_ROWS = 16
MASK_VALUE = -1e30
LOG2E = math.log2(math.e)


def _rmsnorm(x, g):
    return x * lax.rsqrt(jnp.mean(x * x, axis=-1, keepdims=True) + EPS) * g


def _sigmoid(x):
    return jax.nn.sigmoid(x)


def _col_reduce(x, op, groups=8):
    rows, cols = x.shape
    return op(op(x.reshape(groups, rows // groups, cols), axis=0), axis=0, keepdims=True)


def _resident(shape):
    return pl.BlockSpec(shape, lambda *_: (0,) * len(shape), pipeline_mode=pl.Buffered(1))


def _params(*semantics):
    return pltpu.CompilerParams(dimension_semantics=semantics, vmem_limit_bytes=VMEM_LIMIT)


def _cast_weight(src, dst, stage, sem):
    slots, rows = stage.shape[0], dst.shape[0]
    step = max(r for r in range(SUBLANES, stage.shape[1] + 1, SUBLANES) if rows % r == 0)
    n = rows // step

    def copy(k):
        return pltpu.make_async_copy(src.at[pl.ds(k * step, step)],
                                     stage.at[k % slots, pl.ds(0, step)], sem.at[k % slots])

    for k in range(min(slots - 1, n)):
        copy(k).start()
    for k in range(n):
        if k + slots - 1 < n:
            copy(k + slots - 1).start()
        copy(k).wait()
        dst[k * step:(k + 1) * step, :] = stage[k % slots, :step].astype(BF16)


def _conv_layer_kernel(x_ref, ng_ref, w1_hbm, b1_ref, dw_ref, dwb_ref, lng_ref, lnb_ref,
                       w2_hbm, b2_ref, o_ref, ubuf, cbuf, w1_ref, w2_ref, stage1, stage2, sem,
                       *, layer):
    tm, d = x_ref.shape[1], x_ref.shape[2]

    @pl.when((pl.program_id(0) == 0) & (pl.program_id(1) == 0))
    def _():
        _cast_weight(w1_hbm.at[layer], w1_ref, stage1, sem)
        _cast_weight(w2_hbm.at[layer], w2_ref, stage2, sem)

    @pl.when(pl.program_id(1) == 0)
    def _():
        ubuf[0:HALO, :] = jnp.zeros((HALO, d), F32)

    x = x_ref[0]
    h = _rmsnorm(x, ng_ref[...]).astype(BF16)

    def glu(c):
        lo, hi = c * CONV_CHUNK, (c + 1) * CONV_CHUNK
        a = jnp.dot(h, w1_ref[:, lo:hi], preferred_element_type=F32) + b1_ref[:, lo:hi]
        g = jnp.dot(h, w1_ref[:, d + lo:d + hi], preferred_element_type=F32) + b1_ref[:, d + lo:d + hi]
        ubuf[HALO:HALO + tm, lo:hi] = a * _sigmoid(g)

    first = HALO - (CONV_WIDTH - 1)

    def strip(lo):
        col = slice(lo, lo + LANES)
        acc = jnp.broadcast_to(dwb_ref[:, col], (tm, LANES))
        for r in range(SUBLANES):
            rows = tm if r == 0 else tm + SUBLANES
            v = None
            for k in range(first + (r - first) % SUBLANES, first + CONV_WIDTH, SUBLANES):
                term = dw_ref[k - first:k - first + 1, col] * ubuf[k - r:k - r + rows, col]
                v = term if v is None else v + term
            acc = acc + (v if r == 0 else pltpu.roll(v, rows - r, axis=0)[:tm])
        cbuf[:, col] = acc

    n_chunks = d // CONV_CHUNK
    glu(0)
    for c in range(n_chunks):
        if c + 1 < n_chunks:
            glu(c + 1)
        for lo in range(c * CONV_CHUNK, (c + 1) * CONV_CHUNK, LANES):
            strip(lo)

    ubuf[0:HALO, :] = ubuf[tm:tm + HALO, :]

    c = cbuf[...]
    mu = jnp.mean(c, axis=-1, keepdims=True)
    cc = c - mu
    y = cc * lax.rsqrt(jnp.mean(cc * cc, axis=-1, keepdims=True) + EPS) * lng_ref[...] + lnb_ref[...]
    y = (y * _sigmoid(y)).astype(BF16)
    o_ref[0] = x + jnp.dot(y, w2_ref[...], preferred_element_type=F32) + b2_ref[...]


def _conv_layer(x, layer, ng, w1, b1, dw, dwb, lng, lnb, w2, b2):
    b, s, d = x.shape
    tm = ROW_TILE
    tile = pl.BlockSpec((1, tm, d), lambda bi, si: (bi, si, 0))
    hbm = pl.BlockSpec(memory_space=pl.ANY)
    return pl.pallas_call(
        functools.partial(_conv_layer_kernel, layer=layer),
        grid=(b, s // tm),
        in_specs=[tile, _resident((1, d)), hbm, _resident((1, 2 * d)),
                  _resident((CONV_WIDTH, d)), _resident((1, d)), _resident((1, d)),
                  _resident((1, d)), hbm, _resident((1, d))],
        out_specs=tile,
        out_shape=jax.ShapeDtypeStruct(x.shape, F32),
        scratch_shapes=[pltpu.VMEM((HALO + tm, d), F32), pltpu.VMEM((tm, d), F32),
                        pltpu.VMEM(w1.shape[1:], BF16), pltpu.VMEM(w2.shape[1:], BF16),
                        pltpu.VMEM((CAST_SLOTS, d // CAST_CHUNKS, w1.shape[2]), F32),
                        pltpu.VMEM((CAST_SLOTS, d // CAST_CHUNKS, w2.shape[2]), F32),
                        pltpu.SemaphoreType.DMA((CAST_SLOTS,))],
        compiler_params=_params("arbitrary", "arbitrary"),
        name="conv_layer",
    )(x, ng, w1, b1, dw, dwb, lng, lnb, w2, b2)


def _ffn_kernel(*refs, has_attn, final_norm, projs, layers):
    refs = iter(refs)
    x_ref = next(refs)
    if has_attn:
        a_ref, wo_hbm = next(refs), next(refs)
    g_ref, w1_hbm, w3_hbm, w2_hbm = next(refs), next(refs), next(refs), next(refs)
    if final_norm:
        fg_ref = next(refs)
    proj_in = [(next(refs), next(refs)) for _ in projs]
    o_ref = next(refs)
    proj_out = [next(refs) for _ in projs]
    if has_attn:
        wo_ref = next(refs)
    w1_ref, w3_ref, w2_ref = next(refs), next(refs), next(refs)
    proj_w = [next(refs) for _ in projs]
    stage_wide, stage_tall, sem = next(refs), next(refs), next(refs)

    @pl.when((pl.program_id(0) == 0) & (pl.program_id(1) == 0))
    def _():
        hbm = ([wo_hbm] if has_attn else []) + [w1_hbm, w3_hbm, w2_hbm] + [w for _, w in proj_in]
        vmem = ([wo_ref] if has_attn else []) + [w1_ref, w3_ref, w2_ref] + proj_w
        for src, dst, layer in zip(hbm, vmem, layers):
            stage = stage_wide if dst.shape[1] == stage_wide.shape[2] else stage_tall
            _cast_weight(src.at[layer], dst, stage, sem)

    x = x_ref[0]
    if has_attn:
        x = x + lax.dot_general(a_ref[0], wo_ref[...], (((0,), (0,)), ((), ())),
                                preferred_element_type=F32)
    h = _rmsnorm(x, g_ref[...]).astype(BF16)
    h1 = jnp.dot(h, w1_ref[...], preferred_element_type=F32)
    h3 = jnp.dot(h, w3_ref[...], preferred_element_type=F32)
    act = ((h1 * _sigmoid(h1)) * h3).astype(BF16)
    y = x + jnp.dot(act, w2_ref[...], preferred_element_type=F32)
    if final_norm:
        y = _rmsnorm(y, fg_ref[...])
    o_ref[0] = y

    if projs:
        yn = y * lax.rsqrt(jnp.mean(y * y, axis=-1, keepdims=True) + EPS)
    for (pg_ref, _), pw_ref, po_ref, (transposed, scale) in zip(proj_in, proj_w, proj_out, projs):
        hp = (yn * pg_ref[...]).astype(BF16)
        p = jnp.dot(hp, pw_ref[...], preferred_element_type=F32)
        if scale != 1.0:
            p = p * scale
        if transposed:
            po_ref[0] = p.T.astype(BF16)
        else:
            for hd in range(po_ref.shape[1]):
                po_ref[0, hd] = p[:, hd * V_DIM:(hd + 1) * V_DIM].astype(BF16)


def _ffn(x, g, w1, w3, w2, *, attn=None, final_g=None, projs=()):
    b, s, d = x.shape
    f = w1[0].shape[2]
    tm = FFN_TILE
    tile = pl.BlockSpec((1, tm, d), lambda bi, si: (bi, si, 0))
    hbm = pl.BlockSpec(memory_space=pl.ANY)
    args, in_specs, weights = [x], [tile], []
    if attn is not None:
        args += [attn[0], attn[1][0]]
        in_specs += [pl.BlockSpec((1, d, tm), lambda bi, si: (bi, 0, si)), hbm]
        weights.append(attn[1])
    args += [g, w1[0], w3[0], w2[0]]
    in_specs += [_resident((1, d)), hbm, hbm, hbm]
    weights += [w1, w3, w2]
    if final_g is not None:
        args.append(final_g)
        in_specs.append(_resident((1, d)))
    out_specs, out_shapes = [tile], [jax.ShapeDtypeStruct(x.shape, F32)]
    for pg, pw, transposed, _ in projs:
        n = pw[0].shape[2]
        args += [pg, pw[0]]
        in_specs += [_resident((1, d)), hbm]
        weights.append(pw)
        if transposed:
            out_specs.append(pl.BlockSpec((1, n, tm), lambda bi, si: (bi, 0, si)))
            out_shapes.append(jax.ShapeDtypeStruct((b, n, s), BF16))
        else:
            out_specs.append(pl.BlockSpec((1, n // V_DIM, tm, V_DIM), lambda bi, si: (bi, 0, si, 0)))
            out_shapes.append(jax.ShapeDtypeStruct((b, n // V_DIM, s, V_DIM), BF16))
    assert all(w.shape[2] in (d, f) for w, _ in weights) and f % CAST_CHUNKS == 0
    scratch = [pltpu.VMEM(w.shape[1:], BF16) for w, _ in weights]
    scratch += [pltpu.VMEM((CAST_SLOTS, d // CAST_CHUNKS, f), F32),
                pltpu.VMEM((CAST_SLOTS, f // CAST_CHUNKS, d), F32),
                pltpu.SemaphoreType.DMA((CAST_SLOTS,))]
    return pl.pallas_call(
        functools.partial(_ffn_kernel, has_attn=attn is not None, final_norm=final_g is not None,
                          projs=tuple((t, sc) for _, _, t, sc in projs),
                          layers=tuple(layer for _, layer in weights)),
        grid=(b, s // tm),
        in_specs=in_specs,
        out_specs=out_specs,
        out_shape=out_shapes,
        scratch_shapes=scratch,
        compiler_params=_params("arbitrary", "arbitrary"),
        name="ffn",
    )(*args)


def _attn_kernel(slopes_ref, qt_ref, k_ref, vt_ref, lq1_ref, lk1_ref, lq2_ref, lk2_ref, sg_ref,
                 o_ref, bias_scr, qs_scr, s_scr, p_scr, acc_scr, *, lambda_init):
    t_ = ATTN_TILE
    nq = qt_ref.shape[2] // t_
    neg_slope2 = -slopes_ref[pl.program_id(0)] * LOG2E

    @pl.when(pl.program_id(1) == 0)
    def _():
        key = lax.broadcasted_iota(jnp.int32, (t_, t_), 0)
        qry = lax.broadcasted_iota(jnp.int32, (t_, t_), 1)
        bias_rel = neg_slope2 * (qry - key).astype(F32)
        bias_scr[0] = bias_rel
        bias_scr[1] = jnp.where(qry >= key, bias_rel, MASK_VALUE)

    feat = lax.broadcasted_iota(jnp.int32, (V_DIM, t_), 0)
    lam = (jnp.exp(jnp.sum(lq1_ref[...] * lk1_ref[...], axis=-1, keepdims=True))
           - jnp.exp(jnp.sum(lq2_ref[...] * lk2_ref[...], axis=-1, keepdims=True))
           + lambda_init)
    gain = jnp.broadcast_to(sg_ref[...], (V_DIM, t_))

    def tile(idx):
        return slice(idx * t_, (idx + 1) * t_)

    for i in range(nq):
        qt = qt_ref[0, :, tile(i)]
        zero = jnp.zeros_like(qt)
        qs_scr[:, tile(2 * i)] = jnp.where(feat < HEAD_DIM, qt, zero)
        qs_scr[:, tile(2 * i + 1)] = jnp.where(feat >= HEAD_DIM, qt, zero)

    def qk(i, j):
        return jnp.dot(k_ref[0, 0, tile(j), :], qs_scr[:, 2 * i * t_:(2 * i + 2) * t_],
                       preferred_element_type=F32)

    ones = jnp.ones((ONES_ROWS, t_), BF16)

    def pv(i, j, slot, alpha):
        vt1 = jnp.concatenate([vt_ref[0, :, tile(j)], ones], axis=0)
        new = jnp.dot(vt1, p_scr[slot], preferred_element_type=F32)
        acc_scr[i] = new if j == 0 else alpha * acc_scr[i] + new

    def softmax(i, j, slot, m):
        b = bias_scr[1 if i == j else 0]
        z = s_scr[slot] + jnp.concatenate([b, b], axis=1)
        c = neg_slope2 * float((i - j) * t_)
        m_loc = _col_reduce(z, jnp.max) + c
        m_new = m_loc if j == 0 else jnp.maximum(m, m_loc)
        alpha = None if j == 0 else jnp.exp2(m - m_new)
        p_scr[slot] = jnp.exp2(z - (m_new - c)).astype(BF16)
        return m_new, alpha

    def finish(i):
        on = acc_scr[i, :V_DIM, :] * (1.0 / acc_scr[i, V_DIM:V_DIM + 1, :])
        ot = on[:, :t_] - lam * on[:, t_:]
        o = ot * lax.rsqrt(jnp.mean(ot * ot, axis=0, keepdims=True) + EPS) * gain
        o_ref[0, :, tile(i)] = (o * (1.0 - lambda_init)).astype(BF16)

    pairs = [(i, j) for i in range(nq) for j in range(i + 1)]
    s_scr[0] = qk(*pairs[0])
    m = alpha = None
    for t in range(len(pairs) + 1):
        if t + 1 < len(pairs):
            s_scr[(t + 1) % 2] = qk(*pairs[t + 1])
        if t > 0:
            ip, jp = pairs[t - 1]
            pv(ip, jp, (t - 1) % 2, alpha)
            if ip == jp:
                finish(ip)
        if t < len(pairs):
            m, alpha = softmax(*pairs[t], t % 2, m)


def _attention(qt, k, vt, slopes, lq1, lk1, lq2, lk2, sg, lambda_init):
    b, _, s, _ = k.shape
    t_ = ATTN_TILE
    nq = s // t_
    vec = _resident((1, HEAD_DIM))
    return pl.pallas_call(
        functools.partial(_attn_kernel, lambda_init=lambda_init),
        grid=(N_HEADS, b),
        in_specs=[pl.BlockSpec(memory_space=pltpu.SMEM),
                  pl.BlockSpec((1, V_DIM, s), lambda hi, bi: (bi, hi, 0)),
                  pl.BlockSpec((1, 1, s, V_DIM), lambda hi, bi: (bi, hi, 0, 0)),
                  pl.BlockSpec((1, V_DIM, s), lambda hi, bi: (bi, hi, 0)),
                  vec, vec, vec, vec, _resident((V_DIM, 1))],
        out_specs=pl.BlockSpec((1, V_DIM, s), lambda hi, bi: (bi, hi, 0)),
        out_shape=jax.ShapeDtypeStruct(vt.shape, BF16),
        scratch_shapes=[pltpu.VMEM((2, t_, t_), F32), pltpu.VMEM((V_DIM, 2 * s), BF16),
                        pltpu.VMEM((2, t_, 2 * t_), F32), pltpu.VMEM((2, t_, 2 * t_), BF16),
                        pltpu.VMEM((nq, V_DIM + ONES_ROWS, 2 * t_), F32)],
        compiler_params=_params("arbitrary", "arbitrary"),
        name="diff_attention",
    )(slopes, qt, k, vt, lq1, lk1, lq2, lk2, sg)


def kernel(x, conv_norm_g, pw1_w, pw1_b, dw_w, dw_b, conv_ln_g, conv_ln_b, pw2_w, pw2_b,
           kv_norm_g, w_k, w_v, attn_norm_g, w_q, lambda_q1, lambda_k1, lambda_q2, lambda_k2,
           subln_g, w_o, ffn_norm_g, ffn_w1, ffn_w3, ffn_w2, final_norm_g):
    b, s, d = x.shape
    depth = ffn_w1.shape[0]
    n_conv = pw1_w.shape[0]
    row = lambda a: a.reshape(1, -1)
    slopes = jnp.asarray((2.0 ** (-8.0 * np.arange(1, N_HEADS + 1) / N_HEADS)).astype(np.float32))
    scale = HEAD_DIM ** -0.5 * LOG2E

    assert 1 <= n_conv < depth
    w_k, w_v = w_k[None], w_v[None]
    k = vt = qt = None
    for layer in range(depth):
        attn = None
        if layer < n_conv:
            x = _conv_layer(x, layer, row(conv_norm_g[layer]), pw1_w, row(pw1_b[layer]),
                            dw_w[layer], row(dw_b[layer]), row(conv_ln_g[layer]),
                            row(conv_ln_b[layer]), pw2_w, row(pw2_b[layer]))
        else:
            j = layer - n_conv
            lambda_init = 0.8 - 0.6 * math.exp(-0.3 * layer)
            o = _attention(qt, k, vt, slopes, row(lambda_q1[j]), row(lambda_k1[j]),
                           row(lambda_q2[j]), row(lambda_k2[j]), subln_g[j].reshape(-1, 1), lambda_init)
            attn = (o, (w_o, j))
        projs = []
        if layer == n_conv - 1:
            projs += [(row(kv_norm_g), (w_k, 0), False, 1.0), (row(kv_norm_g), (w_v, 0), True, 1.0)]
        if n_conv - 1 <= layer < depth - 1:
            jn = layer + 1 - n_conv
            projs.append((row(attn_norm_g[jn]), (w_q, jn), True, scale))
        outs = _ffn(x, row(ffn_norm_g[layer]), (ffn_w1, layer), (ffn_w3, layer), (ffn_w2, layer),
                    attn=attn, final_g=row(final_norm_g) if layer == depth - 1 else None,
                    projs=projs)
        x = outs[0]
        if layer == n_conv - 1:
            k, vt = outs[1], outs[2]
        if projs:
            qt = outs[-1]
    return x
```

```python
import functools
import math

import jax
import jax.numpy as jnp
import numpy as np
from jax import lax
from jax.experimental import pallas as pl
from jax.experimental.pallas import tpu as pltpu

F32 = jnp.float32
BF16 = jnp.bfloat16

EPS = 1e-6
CONV_WIDTH = 31
N_HEADS = 8
HEAD_DIM = 64
V_DIM = 2 * HEAD_DIM
HALO = 32
LANES = 128
SUBLANES = 8
VMEM_LIMIT = 56 * 1024 * 1024

ROW_TILE = 512
CONV_CHUNK = 256
FFN_TILE = 512
CAST_CHUNKS = 16
CAST_SLOTS = 4
ATTN_TILE = 256
ONES_ROWS = 16
MASK_VALUE = -1e30
LOG2E = math.log2(math.e)


def _rmsnorm(x, g):
    return x * lax.rsqrt(jnp.mean(x * x, axis=-1, keepdims=True) + EPS) * g


def _sigmoid(x):
    return jax.nn.sigmoid(x)


def _col_reduce(x, op, groups=8):
    rows, cols = x.shape
    return op(op(x.reshape(groups, rows // groups, cols), axis=0), axis=0, keepdims=True)


def _resident(shape):
    return pl.BlockSpec(shape, lambda *_: (0,) * len(shape), pipeline_mode=pl.Buffered(1))


def _params(*semantics):
    return pltpu.CompilerParams(dimension_semantics=semantics, vmem_limit_bytes=VMEM_LIMIT)


def _cast_weight(src, dst, stage, sem):
    slots, rows = stage.shape[0], dst.shape[0]
    step = max(r for r in range(SUBLANES, stage.shape[1] + 1, SUBLANES) if rows % r == 0)
    n = rows // step

    def copy(k):
        return pltpu.make_async_copy(src.at[pl.ds(k * step, step)],
                                     stage.at[k % slots, pl.ds(0, step)], sem.at[k % slots])

    for k in range(min(slots - 1, n)):
        copy(k).start()
    for k in range(n):
        if k + slots - 1 < n:
            copy(k + slots - 1).start()
        copy(k).wait()
        dst[k * step:(k + 1) * step, :] = stage[k % slots, :step].astype(BF16)


def _conv_layer_kernel(x_ref, ng_ref, w1_hbm, b1_ref, dw_ref, dwb_ref, lng_ref, lnb_ref,
                       w2_hbm, b2_ref, o_ref, ubuf, cbuf, w1_ref, w2_ref, stage1, stage2, sem,
                       *, layer):
    tm, d = x_ref.shape[1], x_ref.shape[2]

    @pl.when((pl.program_id(0) == 0) & (pl.program_id(1) == 0))
    def _():
        _cast_weight(w1_hbm.at[layer], w1_ref, stage1, sem)
        _cast_weight(w2_hbm.at[layer], w2_ref, stage2, sem)

    @pl.when(pl.program_id(1) == 0)
    def _():
        ubuf[0:HALO, :] = jnp.zeros((HALO, d), F32)

    x = x_ref[0]
    h = _rmsnorm(x, ng_ref[...]).astype(BF16)

    def glu(c):
        lo, hi = c * CONV_CHUNK, (c + 1) * CONV_CHUNK
        a = jnp.dot(h, w1_ref[:, lo:hi], preferred_element_type=F32) + b1_ref[:, lo:hi]
        g = jnp.dot(h, w1_ref[:, d + lo:d + hi], preferred_element_type=F32) + b1_ref[:, d + lo:d + hi]
        ubuf[HALO:HALO + tm, lo:hi] = a * _sigmoid(g)

    first = HALO - (CONV_WIDTH - 1)

    def strip(lo):
        col = slice(lo, lo + LANES)
        acc = jnp.broadcast_to(dwb_ref[:, col], (tm, LANES))
        for r in range(SUBLANES):
            rows = tm if r == 0 else tm + SUBLANES
            v = None
            for k in range(first + (r - first) % SUBLANES, first + CONV_WIDTH, SUBLANES):
                term = dw_ref[k - first:k - first + 1, col] * ubuf[k - r:k - r + rows, col]
                v = term if v is None else v + term
            acc = acc + (v if r == 0 else pltpu.roll(v, rows - r, axis=0)[:tm])
        cbuf[:, col] = acc

    n_chunks = d // CONV_CHUNK
    glu(0)
    for c in range(n_chunks):
        if c + 1 < n_chunks:
            glu(c + 1)
        for lo in range(c * CONV_CHUNK, (c + 1) * CONV_CHUNK, LANES):
            strip(lo)

    ubuf[0:HALO, :] = ubuf[tm:tm + HALO, :]

    c = cbuf[...]
    mu = jnp.mean(c, axis=-1, keepdims=True)
    cc = c - mu
    y = cc * lax.rsqrt(jnp.mean(cc * cc, axis=-1, keepdims=True) + EPS) * lng_ref[...] + lnb_ref[...]
    y = (y * _sigmoid(y)).astype(BF16)
    o_ref[0] = x + jnp.dot(y, w2_ref[...], preferred_element_type=F32) + b2_ref[...]


def _conv_layer(x, layer, ng, w1, b1, dw, dwb, lng, lnb, w2, b2):
    b, s, d = x.shape
    tm = ROW_TILE
    tile = pl.BlockSpec((1, tm, d), lambda bi, si: (bi, si, 0))
    hbm = pl.BlockSpec(memory_space=pl.ANY)
    return pl.pallas_call(
        functools.partial(_conv_layer_kernel, layer=layer),
        grid=(b, s // tm),
        in_specs=[tile, _resident((1, d)), hbm, _resident((1, 2 * d)),
                  _resident((CONV_WIDTH, d)), _resident((1, d)), _resident((1, d)),
                  _resident((1, d)), hbm, _resident((1, d))],
        out_specs=tile,
        out_shape=jax.ShapeDtypeStruct(x.shape, F32),
        scratch_shapes=[pltpu.VMEM((HALO + tm, d), F32), pltpu.VMEM((tm, d), F32),
                        pltpu.VMEM(w1.shape[1:], BF16), pltpu.VMEM(w2.shape[1:], BF16),
                        pltpu.VMEM((CAST_SLOTS, d // CAST_CHUNKS, w1.shape[2]), F32),
                        pltpu.VMEM((CAST_SLOTS, d // CAST_CHUNKS, w2.shape[2]), F32),
                        pltpu.SemaphoreType.DMA((CAST_SLOTS,))],
        compiler_params=_params("arbitrary", "arbitrary"),
        name="conv_layer",
    )(x, ng, w1, b1, dw, dwb, lng, lnb, w2, b2)


def _ffn_kernel(*refs, has_attn, final_norm, projs, layers):
    refs = iter(refs)
    x_ref = next(refs)
    if has_attn:
        a_ref, wo_hbm = next(refs), next(refs)
    g_ref, w1_hbm, w3_hbm, w2_hbm = next(refs), next(refs), next(refs), next(refs)
    if final_norm:
        fg_ref = next(refs)
    proj_in = [(next(refs), next(refs)) for _ in projs]
    o_ref = next(refs)
    proj_out = [next(refs) for _ in projs]
    if has_attn:
        wo_ref = next(refs)
    w1_ref, w3_ref, w2_ref = next(refs), next(refs), next(refs)
    proj_w = [next(refs) for _ in projs]
    stage_wide, stage_tall, sem = next(refs), next(refs), next(refs)

    @pl.when((pl.program_id(0) == 0) & (pl.program_id(1) == 0))
    def _():
        hbm = ([wo_hbm] if has_attn else []) + [w1_hbm, w3_hbm, w2_hbm] + [w for _, w in proj_in]
        vmem = ([wo_ref] if has_attn else []) + [w1_ref, w3_ref, w2_ref] + proj_w
        for src, dst, layer in zip(hbm, vmem, layers):
            stage = stage_wide if dst.shape[1] == stage_wide.shape[2] else stage_tall
            _cast_weight(src.at[layer], dst, stage, sem)

    x = x_ref[0]
    if has_attn:
        x = x + lax.dot_general(a_ref[0], wo_ref[...], (((0,), (0,)), ((), ())),
                                preferred_element_type=F32)
    h = _rmsnorm(x, g_ref[...]).astype(BF16)
    h1 = jnp.dot(h, w1_ref[...], preferred_element_type=F32)
    h3 = jnp.dot(h, w3_ref[...], preferred_element_type=F32)
    act = ((h1 * _sigmoid(h1)) * h3).astype(BF16)
    y = x + jnp.dot(act, w2_ref[...], preferred_element_type=F32)
    if final_norm:
        y = _rmsnorm(y, fg_ref[...])
    o_ref[0] = y

    if projs:
        yn = y * lax.rsqrt(jnp.mean(y * y, axis=-1, keepdims=True) + EPS)
    for (pg_ref, _), pw_ref, po_ref, (transposed, scale) in zip(proj_in, proj_w, proj_out, projs):
        hp = (yn * pg_ref[...]).astype(BF16)
        p = jnp.dot(hp, pw_ref[...], preferred_element_type=F32)
        if scale != 1.0:
            p = p * scale
        if transposed:
            po_ref[0] = p.T.astype(BF16)
        else:
            for hd in range(po_ref.shape[1]):
                po_ref[0, hd] = p[:, hd * V_DIM:(hd + 1) * V_DIM].astype(BF16)


def _ffn(x, g, w1, w3, w2, *, attn=None, final_g=None, projs=()):
    b, s, d = x.shape
    f = w1[0].shape[2]
    tm = FFN_TILE
    tile = pl.BlockSpec((1, tm, d), lambda bi, si: (bi, si, 0))
    hbm = pl.BlockSpec(memory_space=pl.ANY)
    args, in_specs, weights = [x], [tile], []
    if attn is not None:
        args += [attn[0], attn[1][0]]
        in_specs += [pl.BlockSpec((1, d, tm), lambda bi, si: (bi, 0, si)), hbm]
        weights.append(attn[1])
    args += [g, w1[0], w3[0], w2[0]]
    in_specs += [_resident((1, d)), hbm, hbm, hbm]
    weights += [w1, w3, w2]
    if final_g is not None:
        args.append(final_g)
        in_specs.append(_resident((1, d)))
    out_specs, out_shapes = [tile], [jax.ShapeDtypeStruct(x.shape, F32)]
    for pg, pw, transposed, _ in projs:
        n = pw[0].shape[2]
        args += [pg, pw[0]]
        in_specs += [_resident((1, d)), hbm]
        weights.append(pw)
        if transposed:
            out_specs.append(pl.BlockSpec((1, n, tm), lambda bi, si: (bi, 0, si)))
            out_shapes.append(jax.ShapeDtypeStruct((b, n, s), BF16))
        else:
            out_specs.append(pl.BlockSpec((1, n // V_DIM, tm, V_DIM), lambda bi, si: (bi, 0, si, 0)))
            out_shapes.append(jax.ShapeDtypeStruct((b, n // V_DIM, s, V_DIM), BF16))
    assert all(w.shape[2] in (d, f) for w, _ in weights) and f % CAST_CHUNKS == 0
    scratch = [pltpu.VMEM(w.shape[1:], BF16) for w, _ in weights]
    scratch += [pltpu.VMEM((CAST_SLOTS, d // CAST_CHUNKS, f), F32),
                pltpu.VMEM((CAST_SLOTS, f // CAST_CHUNKS, d), F32),
                pltpu.SemaphoreType.DMA((CAST_SLOTS,))]
    return pl.pallas_call(
        functools.partial(_ffn_kernel, has_attn=attn is not None, final_norm=final_g is not None,
                          projs=tuple((t, sc) for _, _, t, sc in projs),
                          layers=tuple(layer for _, layer in weights)),
        grid=(b, s // tm),
        in_specs=in_specs,
        out_specs=out_specs,
        out_shape=out_shapes,
        scratch_shapes=scratch,
        compiler_params=_params("arbitrary", "arbitrary"),
        name="ffn",
    )(*args)


def _attn_kernel(slopes_ref, qt_ref, k_ref, vt_ref, lq1_ref, lk1_ref, lq2_ref, lk2_ref, sg_ref,
                 o_ref, bias_scr, qs_scr, s_scr, p_scr, acc_scr, *, lambda_init):
    t_ = ATTN_TILE
    nq = qt_ref.shape[2] // t_
    neg_slope2 = -slopes_ref[pl.program_id(0)] * LOG2E

    @pl.when(pl.program_id(1) == 0)
    def _():
        key = lax.broadcasted_iota(jnp.int32, (t_, t_), 0)
        qry = lax.broadcasted_iota(jnp.int32, (t_, t_), 1)
        bias_rel = neg_slope2 * (qry - key).astype(F32)
        bias_scr[0] = bias_rel
        bias_scr[1] = jnp.where(qry >= key, bias_rel, MASK_VALUE)

    feat = lax.broadcasted_iota(jnp.int32, (V_DIM, t_), 0)
    lam = (jnp.exp(jnp.sum(lq1_ref[...] * lk1_ref[...], axis=-1, keepdims=True))
           - jnp.exp(jnp.sum(lq2_ref[...] * lk2_ref[...], axis=-1, keepdims=True))
           + lambda_init)
    gain = jnp.broadcast_to(sg_ref[...], (V_DIM, t_))

    def tile(idx):
        return slice(idx * t_, (idx + 1) * t_)

    for i in range(nq):
        qt = qt_ref[0, :, tile(i)]
        zero = jnp.zeros_like(qt)
        qs_scr[:, tile(2 * i)] = jnp.where(feat < HEAD_DIM, qt, zero)
        qs_scr[:, tile(2 * i + 1)] = jnp.where(feat >= HEAD_DIM, qt, zero)

    def qk(i, j):
        return jnp.dot(k_ref[0, 0, tile(j), :], qs_scr[:, 2 * i * t_:(2 * i + 2) * t_],
                       preferred_element_type=F32)

    ones = jnp.ones((ONES_ROWS, t_), BF16)

    def pv(i, j, slot, alpha):
        vt1 = jnp.concatenate([vt_ref[0, :, tile(j)], ones], axis=0)
        new = jnp.dot(vt1, p_scr[slot], preferred_element_type=F32)
        acc_scr[i] = new if j == 0 else alpha * acc_scr[i] + new

    def softmax(i, j, slot, m):
        b = bias_scr[1 if i == j else 0]
        z = s_scr[slot] + jnp.concatenate([b, b], axis=1)
        c = neg_slope2 * float((i - j) * t_)
        m_loc = _col_reduce(z, jnp.max) + c
        m_new = m_loc if j == 0 else jnp.maximum(m, m_loc)
        alpha = None if j == 0 else jnp.exp2(m - m_new)
        p_scr[slot] = jnp.exp2(z - (m_new - c)).astype(BF16)
        return m_new, alpha

    def finish(i):
        on = acc_scr[i, :V_DIM, :] * (1.0 / acc_scr[i, V_DIM:V_DIM + 1, :])
        ot = on[:, :t_] - lam * on[:, t_:]
        o = ot * lax.rsqrt(jnp.mean(ot * ot, axis=0, keepdims=True) + EPS) * gain
        o_ref[0, :, tile(i)] = (o * (1.0 - lambda_init)).astype(BF16)

    pairs = [(i, j) for i in range(nq) for j in range(i + 1)]
    s_scr[0] = qk(*pairs[0])
    m = alpha = None
    for t in range(len(pairs) + 1):
        if t + 1 < len(pairs):
            s_scr[(t + 1) % 2] = qk(*pairs[t + 1])
        if t > 0:
            ip, jp = pairs[t - 1]
            pv(ip, jp, (t - 1) % 2, alpha)
            if ip == jp:
                finish(ip)
        if t < len(pairs):
            m, alpha = softmax(*pairs[t], t % 2, m)


def _attention(qt, k, vt, slopes, lq1, lk1, lq2, lk2, sg, lambda_init):
    b, _, s, _ = k.shape
    t_ = ATTN_TILE
    nq = s // t_
    vec = _resident((1, HEAD_DIM))
    return pl.pallas_call(
        functools.partial(_attn_kernel, lambda_init=lambda_init),
        grid=(N_HEADS, b),
        in_specs=[pl.BlockSpec(memory_space=pltpu.SMEM),
                  pl.BlockSpec((1, V_DIM, s), lambda hi, bi: (bi, hi, 0)),
                  pl.BlockSpec((1, 1, s, V_DIM), lambda hi, bi: (bi, hi, 0, 0)),
                  pl.BlockSpec((1, V_DIM, s), lambda hi, bi: (bi, hi, 0)),
                  vec, vec, vec, vec, _resident((V_DIM, 1))],
        out_specs=pl.BlockSpec((1, V_DIM, s), lambda hi, bi: (bi, hi, 0)),
        out_shape=jax.ShapeDtypeStruct(vt.shape, BF16),
        scratch_shapes=[pltpu.VMEM((2, t_, t_), F32), pltpu.VMEM((V_DIM, 2 * s), BF16),
                        pltpu.VMEM((2, t_, 2 * t_), F32), pltpu.VMEM((2, t_, 2 * t_), BF16),
                        pltpu.VMEM((nq, V_DIM + ONES_ROWS, 2 * t_), F32)],
        compiler_params=_params("arbitrary", "arbitrary"),
        name="diff_attention",
    )(slopes, qt, k, vt, lq1, lk1, lq2, lk2, sg)


def kernel(x, conv_norm_g, pw1_w, pw1_b, dw_w, dw_b, conv_ln_g, conv_ln_b, pw2_w, pw2_b,
           kv_norm_g, w_k, w_v, attn_norm_g, w_q, lambda_q1, lambda_k1, lambda_q2, lambda_k2,
           subln_g, w_o, ffn_norm_g, ffn_w1, ffn_w3, ffn_w2, final_norm_g):
    b, s, d = x.shape
    depth = ffn_w1.shape[0]
    n_conv = pw1_w.shape[0]
    row = lambda a: a.reshape(1, -1)
    slopes = jnp.asarray((2.0 ** (-8.0 * np.arange(1, N_HEADS + 1) / N_HEADS)).astype(np.float32))
    scale = HEAD_DIM ** -0.5 * LOG2E

    assert 1 <= n_conv < depth
    w_k, w_v = w_k[None], w_v[None]
    k = vt = qt = None
    for layer in range(depth):
        attn = None
        if layer < n_conv:
            x = _conv_layer(x, layer, row(conv_norm_g[layer]), pw1_w, row(pw1_b[layer]),
                            dw_w[layer], row(dw_b[layer]), row(conv_ln_g[layer]),
                            row(conv_ln_b[layer]), pw2_w, row(pw2_b[layer]))
        else:
            j = layer - n_conv
            lambda_init = 0.8 - 0.6 * math.exp(-0.3 * layer)
            o = _attention(qt, k, vt, slopes, row(lambda_q1[j]), row(lambda_k1[j]),
                           row(lambda_q2[j]), row(lambda_k2[j]), subln_g[j].reshape(-1, 1), lambda_init)
            attn = (o, (w_o, j))
        projs = []
        if layer == n_conv - 1:
            projs += [(row(kv_norm_g), (w_k, 0), False, 1.0), (row(kv_norm_g), (w_v, 0), True, 1.0)]
        if n_conv - 1 <= layer < depth - 1:
            jn = layer + 1 - n_conv
            projs.append((row(attn_norm_g[jn]), (w_q, jn), True, scale))
        outs = _ffn(x, row(ffn_norm_g[layer]), (ffn_w1, layer), (ffn_w3, layer), (ffn_w2, layer),
                    attn=attn, final_g=row(final_norm_g) if layer == depth - 1 else None,
                    projs=projs)
        x = outs[0]
        if layer == n_conv - 1:
            k, vt = outs[1], outs[2]
        if projs:
            qt = outs[-1]
    return x
```

```python
import functools
import math

import jax
import jax.numpy as jnp
import numpy as np
from jax import lax
from jax.experimental import pallas as pl
from jax.experimental.pallas import tpu as pltpu

F32 = jnp.float32
BF16 = jnp.bfloat16

EPS = 1e-6
CONV_WIDTH = 31
N_HEADS = 8
HEAD_DIM = 64
V_DIM = 2 * HEAD_DIM
HALO = 32
LANES = 128
SUBLANES = 8
VMEM_LIMIT = 56 * 1024 * 1024

ROW_TILE = 1024
CONV_CHUNK = 256
FFN_TILE = 512
CAST_CHUNKS = 16
CAST_SLOTS = 4
ATTN_TILE = 256
ONES_ROWS = 16
MASK_VALUE = -1e30
LOG2E = math.log2(math.e)


def _rmsnorm(x, g):
    return x * lax.rsqrt(jnp.mean(x * x, axis=-1, keepdims=True) + EPS) * g


def _sigmoid(x):
    return jax.nn.sigmoid(x)


def _col_reduce(x, op, groups=8):
    rows, cols = x.shape
    return op(op(x.reshape(groups, rows // groups, cols), axis=0), axis=0, keepdims=True)


def _resident(shape):
    return pl.BlockSpec(shape, lambda *_: (0,) * len(shape), pipeline_mode=pl.Buffered(1))


def _params(*semantics):
    return pltpu.CompilerParams(dimension_semantics=semantics, vmem_limit_bytes=VMEM_LIMIT)


def _cast_weight(src, dst, stage, sem):
    slots, rows = stage.shape[0], dst.shape[0]
    step = max(r for r in range(SUBLANES, stage.shape[1] + 1, SUBLANES) if rows % r == 0)
    n = rows // step

    def copy(k):
        return pltpu.make_async_copy(src.at[pl.ds(k * step, step)],
                                     stage.at[k % slots, pl.ds(0, step)], sem.at[k % slots])

    for k in range(min(slots - 1, n)):
        copy(k).start()
    for k in range(n):
        if k + slots - 1 < n:
            copy(k + slots - 1).start()
        copy(k).wait()
        dst[k * step:(k + 1) * step, :] = stage[k % slots, :step].astype(BF16)


def _conv_layer_kernel(x_ref, ng_ref, w1_hbm, b1_ref, dw_ref, dwb_ref, lng_ref, lnb_ref,
                       w2_hbm, b2_ref, o_ref, ubuf, cbuf, w1_ref, w2_ref, stage1, stage2, sem,
                       *, layer):
    tm, d = x_ref.shape[1], x_ref.shape[2]

    @pl.when((pl.program_id(0) == 0) & (pl.program_id(1) == 0))
    def _():
        _cast_weight(w1_hbm.at[layer], w1_ref, stage1, sem)
        _cast_weight(w2_hbm.at[layer], w2_ref, stage2, sem)

    @pl.when(pl.program_id(1) == 0)
    def _():
        ubuf[0:HALO, :] = jnp.zeros((HALO, d), F32)

    x = x_ref[0]
    h = _rmsnorm(x, ng_ref[...]).astype(BF16)

    def glu(c):
        lo, hi = c * CONV_CHUNK, (c + 1) * CONV_CHUNK
        a = jnp.dot(h, w1_ref[:, lo:hi], preferred_element_type=F32) + b1_ref[:, lo:hi]
        g = jnp.dot(h, w1_ref[:, d + lo:d + hi], preferred_element_type=F32) + b1_ref[:, d + lo:d + hi]
        ubuf[HALO:HALO + tm, lo:hi] = a * _sigmoid(g)

    first = HALO - (CONV_WIDTH - 1)

    def strip(lo):
        col = slice(lo, lo + LANES)
        acc = jnp.broadcast_to(dwb_ref[:, col], (tm, LANES))
        for r in range(SUBLANES):
            rows = tm if r == 0 else tm + SUBLANES
            v = None
            for k in range(first + (r - first) % SUBLANES, first + CONV_WIDTH, SUBLANES):
                term = dw_ref[k - first:k - first + 1, col] * ubuf[k - r:k - r + rows, col]
                v = term if v is None else v + term
            acc = acc + (v if r == 0 else pltpu.roll(v, rows - r, axis=0)[:tm])
        cbuf[:, col] = acc

    n_chunks = d // CONV_CHUNK
    glu(0)
    for c in range(n_chunks):
        if c + 1 < n_chunks:
            glu(c + 1)
        for lo in range(c * CONV_CHUNK, (c + 1) * CONV_CHUNK, LANES):
            strip(lo)

    ubuf[0:HALO, :] = ubuf[tm:tm + HALO, :]

    c = cbuf[...]
    mu = jnp.mean(c, axis=-1, keepdims=True)
    cc = c - mu
    y = cc * lax.rsqrt(jnp.mean(cc * cc, axis=-1, keepdims=True) + EPS) * lng_ref[...] + lnb_ref[...]
    y = (y * _sigmoid(y)).astype(BF16)
    o_ref[0] = x + jnp.dot(y, w2_ref[...], preferred_element_type=F32) + b2_ref[...]


def _conv_layer(x, layer, ng, w1, b1, dw, dwb, lng, lnb, w2, b2):
    b, s, d = x.shape
    tm = ROW_TILE
    tile = pl.BlockSpec((1, tm, d), lambda bi, si: (bi, si, 0))
    hbm = pl.BlockSpec(memory_space=pl.ANY)
    return pl.pallas_call(
        functools.partial(_conv_layer_kernel, layer=layer),
        grid=(b, s // tm),
        in_specs=[tile, _resident((1, d)), hbm, _resident((1, 2 * d)),
                  _resident((CONV_WIDTH, d)), _resident((1, d)), _resident((1, d)),
                  _resident((1, d)), hbm, _resident((1, d))],
        out_specs=tile,
        out_shape=jax.ShapeDtypeStruct(x.shape, F32),
        scratch_shapes=[pltpu.VMEM((HALO + tm, d), F32), pltpu.VMEM((tm, d), F32),
                        pltpu.VMEM(w1.shape[1:], BF16), pltpu.VMEM(w2.shape[1:], BF16),
                        pltpu.VMEM((CAST_SLOTS, d // CAST_CHUNKS, w1.shape[2]), F32),
                        pltpu.VMEM((CAST_SLOTS, d // CAST_CHUNKS, w2.shape[2]), F32),
                        pltpu.SemaphoreType.DMA((CAST_SLOTS,))],
        compiler_params=_params("arbitrary", "arbitrary"),
        name="conv_layer",
    )(x, ng, w1, b1, dw, dwb, lng, lnb, w2, b2)


def _ffn_kernel(*refs, has_attn, final_norm, projs, layers):
    refs = iter(refs)
    x_ref = next(refs)
    if has_attn:
        a_ref, wo_hbm = next(refs), next(refs)
    g_ref, w1_hbm, w3_hbm, w2_hbm = next(refs), next(refs), next(refs), next(refs)
    if final_norm:
        fg_ref = next(refs)
    proj_in = [(next(refs), next(refs)) for _ in projs]
    o_ref = next(refs)
    proj_out = [next(refs) for _ in projs]
    if has_attn:
        wo_ref = next(refs)
    w1_ref, w3_ref, w2_ref = next(refs), next(refs), next(refs)
    proj_w = [next(refs) for _ in projs]
    stage_wide, stage_tall, sem = next(refs), next(refs), next(refs)

    @pl.when((pl.program_id(0) == 0) & (pl.program_id(1) == 0))
    def _():
        hbm = ([wo_hbm] if has_attn else []) + [w1_hbm, w3_hbm, w2_hbm] + [w for _, w in proj_in]
        vmem = ([wo_ref] if has_attn else []) + [w1_ref, w3_ref, w2_ref] + proj_w
        for src, dst, layer in zip(hbm, vmem, layers):
            stage = stage_wide if dst.shape[1] == stage_wide.shape[2] else stage_tall
            _cast_weight(src.at[layer], dst, stage, sem)

    x = x_ref[0]
    if has_attn:
        x = x + lax.dot_general(a_ref[0], wo_ref[...], (((0,), (0,)), ((), ())),
                                preferred_element_type=F32)
    h = _rmsnorm(x, g_ref[...]).astype(BF16)
    h1 = jnp.dot(h, w1_ref[...], preferred_element_type=F32)
    h3 = jnp.dot(h, w3_ref[...], preferred_element_type=F32)
    act = ((h1 * _sigmoid(h1)) * h3).astype(BF16)
    y = x + jnp.dot(act, w2_ref[...], preferred_element_type=F32)
    if final_norm:
        y = _rmsnorm(y, fg_ref[...])
    o_ref[0] = y

    if projs:
        yn = y * lax.rsqrt(jnp.mean(y * y, axis=-1, keepdims=True) + EPS)
    for (pg_ref, _), pw_ref, po_ref, (transposed, scale) in zip(proj_in, proj_w, proj_out, projs):
        hp = (yn * pg_ref[...]).astype(BF16)
        p = jnp.dot(hp, pw_ref[...], preferred_element_type=F32)
        if scale != 1.0:
            p = p * scale
        if transposed:
            po_ref[0] = p.T.astype(BF16)
        else:
            for hd in range(po_ref.shape[1]):
                po_ref[0, hd] = p[:, hd * V_DIM:(hd + 1) * V_DIM].astype(BF16)


def _ffn(x, g, w1, w3, w2, *, attn=None, final_g=None, projs=()):
    b, s, d = x.shape
    f = w1[0].shape[2]
    tm = FFN_TILE
    tile = pl.BlockSpec((1, tm, d), lambda bi, si: (bi, si, 0))
    hbm = pl.BlockSpec(memory_space=pl.ANY)
    args, in_specs, weights = [x], [tile], []
    if attn is not None:
        args += [attn[0], attn[1][0]]
        in_specs += [pl.BlockSpec((1, d, tm), lambda bi, si: (bi, 0, si)), hbm]
        weights.append(attn[1])
    args += [g, w1[0], w3[0], w2[0]]
    in_specs += [_resident((1, d)), hbm, hbm, hbm]
    weights += [w1, w3, w2]
    if final_g is not None:
        args.append(final_g)
        in_specs.append(_resident((1, d)))
    out_specs, out_shapes = [tile], [jax.ShapeDtypeStruct(x.shape, F32)]
    for pg, pw, transposed, _ in projs:
        n = pw[0].shape[2]
        args += [pg, pw[0]]
        in_specs += [_resident((1, d)), hbm]
        weights.append(pw)
        if transposed:
            out_specs.append(pl.BlockSpec((1, n, tm), lambda bi, si: (bi, 0, si)))
            out_shapes.append(jax.ShapeDtypeStruct((b, n, s), BF16))
        else:
            out_specs.append(pl.BlockSpec((1, n // V_DIM, tm, V_DIM), lambda bi, si: (bi, 0, si, 0)))
            out_shapes.append(jax.ShapeDtypeStruct((b, n // V_DIM, s, V_DIM), BF16))
    assert all(w.shape[2] in (d, f) for w, _ in weights) and f % CAST_CHUNKS == 0
    scratch = [pltpu.VMEM(w.shape[1:], BF16) for w, _ in weights]
    scratch += [pltpu.VMEM((CAST_SLOTS, d // CAST_CHUNKS, f), F32),
                pltpu.VMEM((CAST_SLOTS, f // CAST_CHUNKS, d), F32),
                pltpu.SemaphoreType.DMA((CAST_SLOTS,))]
    return pl.pallas_call(
        functools.partial(_ffn_kernel, has_attn=attn is not None, final_norm=final_g is not None,
                          projs=tuple((t, sc) for _, _, t, sc in projs),
                          layers=tuple(layer for _, layer in weights)),
        grid=(b, s // tm),
        in_specs=in_specs,
        out_specs=out_specs,
        out_shape=out_shapes,
        scratch_shapes=scratch,
        compiler_params=_params("arbitrary", "arbitrary"),
        name="ffn",
    )(*args)


def _attn_kernel(slopes_ref, qt_ref, k_ref, vt_ref, lq1_ref, lk1_ref, lq2_ref, lk2_ref, sg_ref,
                 o_ref, bias_scr, qs_scr, s_scr, p_scr, acc_scr, *, lambda_init):
    t_ = ATTN_TILE
    nq = qt_ref.shape[2] // t_
    neg_slope2 = -slopes_ref[pl.program_id(0)] * LOG2E

    @pl.when(pl.program_id(1) == 0)
    def _():
        key = lax.broadcasted_iota(jnp.int32, (t_, t_), 0)
        qry = lax.broadcasted_iota(jnp.int32, (t_, t_), 1)
        bias_rel = neg_slope2 * (qry - key).astype(F32)
        bias_scr[0] = bias_rel
        bias_scr[1] = jnp.where(qry >= key, bias_rel, MASK_VALUE)

    feat = lax.broadcasted_iota(jnp.int32, (V_DIM, t_), 0)
    lam = (jnp.exp(jnp.sum(lq1_ref[...] * lk1_ref[...], axis=-1, keepdims=True))
           - jnp.exp(jnp.sum(lq2_ref[...] * lk2_ref[...], axis=-1, keepdims=True))
           + lambda_init)
    gain = jnp.broadcast_to(sg_ref[...], (V_DIM, t_))

    def tile(idx):
        return slice(idx * t_, (idx + 1) * t_)

    for i in range(nq):
        qt = qt_ref[0, :, tile(i)]
        zero = jnp.zeros_like(qt)
        qs_scr[:, tile(2 * i)] = jnp.where(feat < HEAD_DIM, qt, zero)
        qs_scr[:, tile(2 * i + 1)] = jnp.where(feat >= HEAD_DIM, qt, zero)

    def qk(i, j):
        return jnp.dot(k_ref[0, 0, tile(j), :], qs_scr[:, 2 * i * t_:(2 * i + 2) * t_],
                       preferred_element_type=F32)

    ones = jnp.ones((ONES_ROWS, t_), BF16)

    def pv(i, j, slot, alpha):
        vt1 = jnp.concatenate([vt_ref[0, :, tile(j)], ones], axis=0)
        new = jnp.dot(vt1, p_scr[slot], preferred_element_type=F32)
        acc_scr[i] = new if j == 0 else alpha * acc_scr[i] + new

    def softmax(i, j, slot, m):
        b = bias_scr[1 if i == j else 0]
        z = s_scr[slot] + jnp.concatenate([b, b], axis=1)
        c = neg_slope2 * float((i - j) * t_)
        m_loc = _col_reduce(z, jnp.max) + c
        m_new = m_loc if j == 0 else jnp.maximum(m, m_loc)
        alpha = None if j == 0 else jnp.exp2(m - m_new)
        p_scr[slot] = jnp.exp2(z - (m_new - c)).astype(BF16)
        return m_new, alpha

    def finish(i):
        on = acc_scr[i, :V_DIM, :] * (1.0 / acc_scr[i, V_DIM:V_DIM + 1, :])
        ot = on[:, :t_] - lam * on[:, t_:]
        o = ot * lax.rsqrt(jnp.mean(ot * ot, axis=0, keepdims=True) + EPS) * gain
        o_ref[0, :, tile(i)] = (o * (1.0 - lambda_init)).astype(BF16)

    pairs = [(i, j) for i in range(nq) for j in range(i + 1)]
    s_scr[0] = qk(*pairs[0])
    m = alpha = None
    for t in range(len(pairs) + 1):
        if t + 1 < len(pairs):
            s_scr[(t + 1) % 2] = qk(*pairs[t + 1])
        if t > 0:
            ip, jp = pairs[t - 1]
            pv(ip, jp, (t - 1) % 2, alpha)
            if ip == jp:
                finish(ip)
        if t < len(pairs):
            m, alpha = softmax(*pairs[t], t % 2, m)


def _attention(qt, k, vt, slopes, lq1, lk1, lq2, lk2, sg, lambda_init):
    b, _, s, _ = k.shape
    t_ = ATTN_TILE
    nq = s // t_
    vec = _resident((1, HEAD_DIM))
    return pl.pallas_call(
        functools.partial(_attn_kernel, lambda_init=lambda_init),
        grid=(N_HEADS, b),
        in_specs=[pl.BlockSpec(memory_space=pltpu.SMEM),
                  pl.BlockSpec((1, V_DIM, s), lambda hi, bi: (bi, hi, 0)),
                  pl.BlockSpec((1, 1, s, V_DIM), lambda hi, bi: (bi, hi, 0, 0)),
                  pl.BlockSpec((1, V_DIM, s), lambda hi, bi: (bi, hi, 0)),
                  vec, vec, vec, vec, _resident((V_DIM, 1))],
        out_specs=pl.BlockSpec((1, V_DIM, s), lambda hi, bi: (bi, hi, 0)),
        out_shape=jax.ShapeDtypeStruct(vt.shape, BF16),
        scratch_shapes=[pltpu.VMEM((2, t_, t_), F32), pltpu.VMEM((V_DIM, 2 * s), BF16),
                        pltpu.VMEM((2, t_, 2 * t_), F32), pltpu.VMEM((2, t_, 2 * t_), BF16),
                        pltpu.VMEM((nq, V_DIM + ONES_ROWS, 2 * t_), F32)],
        compiler_params=_params("arbitrary", "arbitrary"),
        name="diff_attention",
    )(slopes, qt, k, vt, lq1, lk1, lq2, lk2, sg)


def kernel(x, conv_norm_g, pw1_w, pw1_b, dw_w, dw_b, conv_ln_g, conv_ln_b, pw2_w, pw2_b,
           kv_norm_g, w_k, w_v, attn_norm_g, w_q, lambda_q1, lambda_k1, lambda_q2, lambda_k2,
           subln_g, w_o, ffn_norm_g, ffn_w1, ffn_w3, ffn_w2, final_norm_g):
    b, s, d = x.shape
    depth = ffn_w1.shape[0]
    n_conv = pw1_w.shape[0]
    row = lambda a: a.reshape(1, -1)
    slopes = jnp.asarray((2.0 ** (-8.0 * np.arange(1, N_HEADS + 1) / N_HEADS)).astype(np.float32))
    scale = HEAD_DIM ** -0.5 * LOG2E

    assert 1 <= n_conv < depth
    w_k, w_v = w_k[None], w_v[None]
    k = vt = qt = None
    for layer in range(depth):
        attn = None
        if layer < n_conv:
            x = _conv_layer(x, layer, row(conv_norm_g[layer]), pw1_w, row(pw1_b[layer]),
                            dw_w[layer], row(dw_b[layer]), row(conv_ln_g[layer]),
                            row(conv_ln_b[layer]), pw2_w, row(pw2_b[layer]))
        else:
            j = layer - n_conv
            lambda_init = 0.8 - 0.6 * math.exp(-0.3 * layer)
            o = _attention(qt, k, vt, slopes, row(lambda_q1[j]), row(lambda_k1[j]),
                           row(lambda_q2[j]), row(lambda_k2[j]), subln_g[j].reshape(-1, 1), lambda_init)
            attn = (o, (w_o, j))
        projs = []
        if layer == n_conv - 1:
            projs += [(row(kv_norm_g), (w_k, 0), False, 1.0), (row(kv_norm_g), (w_v, 0), True, 1.0)]
        if n_conv - 1 <= layer < depth - 1:
            jn = layer + 1 - n_conv
            projs.append((row(attn_norm_g[jn]), (w_q, jn), True, scale))
        outs = _ffn(x, row(ffn_norm_g[layer]), (ffn_w1, layer), (ffn_w3, layer), (ffn_w2, layer),
                    attn=attn, final_g=row(final_norm_g) if layer == depth - 1 else None,
                    projs=projs)
        x = outs[0]
        if layer == n_conv - 1:
            k, vt = outs[1], outs[2]
        if projs:
            qt = outs[-1]
    return x
```
